```python
import math
import jax, jax.numpy as jnp
from jax import lax
import numpy as np

D_MODEL = 1024
BATCH = 2
SEQ = 8192
DEPTH = 4
DEC_BATCH = 32
DEC_SEQ = 32
PAST_LEN = 4096

CHUNK = 64
D_A = D_MODEL
CONV_A_WIDTH = 31
D_INNER = 2 * D_MODEL
SSM_HEAD_DIM = 64
SSM_HEADS = D_INNER // SSM_HEAD_DIM
SSM_GROUPS = 4
SSM_STATE = 128
CONV_B_WIDTH = 4
D_XBC = D_INNER + 2 * SSM_GROUPS * SSM_STATE
PLE_DIM = 256
IN_SIZES = (D_A, D_A, D_A, D_INNER, D_XBC, SSM_HEADS, D_MODEL, D_MODEL, D_MODEL)
D_IN_TOTAL = sum(IN_SIZES)
DEEPNORM_ALPHA = (2 * DEPTH) ** 0.25
DEEPNORM_BETA = (8 * DEPTH) ** -0.25
LN_EPS = 1e-5

kernel_name = "conformer_mamba2_gated_stream_step"


def layer_norm(x, g, b):
    xf = x.astype(jnp.float32)
    mu = jnp.mean(xf, axis=-1, keepdims=True)
    var = jnp.mean(jnp.square(xf - mu), axis=-1, keepdims=True)
    return ((xf - mu) * lax.rsqrt(var + LN_EPS) * g + b).astype(x.dtype)


def gated_rmsnorm(y, z, w):
    bsz, L, _ = y.shape
    g = (y * jax.nn.silu(z.astype(jnp.float32))).reshape(bsz, L, SSM_GROUPS, -1)
    g = g * lax.rsqrt(jnp.mean(jnp.square(g), axis=-1, keepdims=True) + LN_EPS)
    return (g.reshape(bsz, L, D_INNER) * w).astype(z.dtype)


def causal_dwconv(u, buf, w, bias):
    K, C = w.shape
    up = jnp.concatenate([buf.astype(u.dtype), u], axis=1)
    y = lax.conv_general_dilated(up, w[:, None, :].astype(u.dtype), window_strides=(1,),
                                 padding='VALID', dimension_numbers=('NWC', 'WIO', 'NWC'),
                                 feature_group_count=C)
    return y + bias, up[:, up.shape[1] - (K - 1):]


def ssd_scan(xh, dt, A, Bm, Cm, h0):
    bsz, L = xh.shape[:2]
    nc = -(-L // CHUNK)
    pad = nc * CHUNK - L
    hg = SSM_HEADS // SSM_GROUPS

    def padt(t):
        return jnp.pad(t, [(0, 0), (0, pad)] + [(0, 0)] * (t.ndim - 2))

    xdt = padt(xh * dt[..., None]).reshape(bsz, nc, CHUNK, SSM_GROUPS, hg, SSM_HEAD_DIM)
    a = padt(dt * A).reshape(bsz, nc, CHUNK, SSM_GROUPS, hg)
    Bc = padt(Bm).reshape(bsz, nc, CHUNK, SSM_GROUPS, SSM_STATE)
    Cc = padt(Cm).reshape(bsz, nc, CHUNK, SSM_GROUPS, SSM_STATE)
    a_cum = jnp.cumsum(a, axis=2)
    mask = jnp.tril(jnp.ones((CHUNK, CHUNK), dtype=bool))[:, :, None, None]
    seg = a_cum[:, :, :, None] - a_cum[:, :, None, :]
    Lm = jnp.exp(jnp.where(mask, seg, -jnp.inf))
    CB = jnp.einsum('bclgn,bcsgn->bclsg', Cc, Bc)
    y_diag = jnp.einsum('bclsg,bclsgh,bcsghp->bclghp', CB, Lm, xdt)
    decay_states = jnp.exp(a_cum[:, :, -1:] - a_cum)
    states = jnp.einsum('bclgn,bclgh,bclghp->bcghpn', Bc, decay_states, xdt)
    chunk_decay = jnp.exp(a_cum[:, :, -1])

    def step(h, inp):
        dec, st = inp
        return h * dec[..., None, None] + st, h

    h_init = h0.reshape(bsz, SSM_GROUPS, hg, SSM_HEAD_DIM, SSM_STATE)
    h_final, h_starts = lax.scan(step, h_init, (jnp.swapaxes(chunk_decay, 0, 1), jnp.swapaxes(states, 0, 1)))
    h_starts = jnp.swapaxes(h_starts, 0, 1)
    y_off = jnp.einsum('bclgn,bcghpn,bclgh->bclghp', Cc, h_starts, jnp.exp(a_cum))
    y = (y_diag + y_off).reshape(bsz, nc * CHUNK, SSM_HEADS, SSM_HEAD_DIM)[:, :L]
    return y, h_final.reshape(bsz, SSM_HEADS, SSM_HEAD_DIM, SSM_STATE)


def trunk_layer(x, p, buf_a, buf_b, h0, w_in, b_in, conv_a_w, conv_a_b, norm_a_g, norm_a_b, w_a_out,
                conv_b_w, conv_b_b, dt_bias, a_log, d_skip, gnorm_w, w_b_out, w_out, w_ple, ln_g, ln_b):
    bsz, L, _ = x.shape
    proj = x @ w_in + b_in
    idx = np.cumsum(IN_SIZES)[:-1].tolist()
    a_val, a_glu, a_gate, z, xbc, dt_raw, g_a, g_b, g_p = jnp.split(proj, idx, axis=-1)

    u = a_val * jax.nn.sigmoid(a_glu)
    u, new_buf_a = causal_dwconv(u, buf_a, conv_a_w, conv_a_b)
    u = jax.nn.silu(layer_norm(u, norm_a_g, norm_a_b)) * jax.nn.silu(a_gate)
    out_a = u @ w_a_out

    xbc, new_buf_b = causal_dwconv(xbc, buf_b, conv_b_w, conv_b_b)
    xbc = jax.nn.silu(xbc)
    xs, Bm, Cm = jnp.split(xbc, [D_INNER, D_INNER + SSM_GROUPS * SSM_STATE], axis=-1)
    xs_h = xs.reshape(bsz, L, SSM_HEADS, SSM_HEAD_DIM).astype(jnp.float32)
    dt = jax.nn.softplus(dt_raw.astype(jnp.float32) + dt_bias.astype(jnp.float32))
    A = -jnp.exp(a_log.astype(jnp.float32))
    y, h_new = ssd_scan(xs_h, dt, A,
                        Bm.reshape(bsz, L, SSM_GROUPS, SSM_STATE).astype(jnp.float32),
                        Cm.reshape(bsz, L, SSM_GROUPS, SSM_STATE).astype(jnp.float32),
                        h0.astype(jnp.float32))
    y = (y + d_skip.astype(jnp.float32)[:, None] * xs_h).reshape(bsz, L, D_INNER)
    out_b = gated_rmsnorm(y, z, gnorm_w) @ w_b_out

    merged = jax.nn.sigmoid(g_a) * out_a + jax.nn.sigmoid(g_b) * out_b
    ple = jax.nn.sigmoid(g_p) * (p.astype(x.dtype) @ w_ple)
    x_new = layer_norm(DEEPNORM_ALPHA * x + merged @ w_out + ple, ln_g, ln_b)
    return x_new, new_buf_a, new_buf_b, h_new


def run_trunk(x, p, bufs_a, bufs_b, hs, weights):
    new_a, new_b, new_h = [], [], []
    for i in range(DEPTH):
        x, ba, bb, h = trunk_layer(x, p[i], bufs_a[i], bufs_b[i], hs[i], *[w[i] for w in weights])
        new_a.append(ba)
        new_b.append(bb)
        new_h.append(h)
    return x, jnp.stack(new_a), jnp.stack(new_b), jnp.stack(new_h)


def setup_inputs(seed: int = 0) -> dict:
    key = jax.random.key(seed)
    ks = jax.random.split(key, 32)
    f32 = jnp.float32
    nrm = lambda k, shape, s: (jax.random.normal(k, shape, f32) * s)
    dt0 = jnp.exp(jax.random.uniform(ks[0], (DEPTH, SSM_HEADS), f32) * (math.log(0.1) - math.log(0.001)) + math.log(0.001))
    return {
        "x_prompt": nrm(ks[1], (BATCH, SEQ, D_MODEL), 1.0),
        "x_sample": nrm(ks[2], (DEC_BATCH, DEC_SEQ, D_MODEL), 1.0),
        "cache_conv_a": nrm(ks[3], (DEPTH, DEC_BATCH, CONV_A_WIDTH - 1, D_A), 0.5),
        "cache_conv_b": nrm(ks[4], (DEPTH, DEC_BATCH, CONV_B_WIDTH - 1, D_XBC), 0.5),
        "state_ssm": nrm(ks[5], (DEPTH, DEC_BATCH, SSM_HEADS, SSM_HEAD_DIM, SSM_STATE), 0.1),
        "p_prompt": nrm(ks[6], (DEPTH, BATCH, SEQ, PLE_DIM), 1.0),
        "p_sample": nrm(ks[7], (DEPTH, DEC_BATCH, DEC_SEQ, PLE_DIM), 1.0),
        "w_in": nrm(ks[8], (DEPTH, D_MODEL, D_IN_TOTAL), D_MODEL ** -0.5),
        "b_in": nrm(ks[9], (DEPTH, D_IN_TOTAL), 0.01),
        "conv_a_w": nrm(ks[10], (DEPTH, CONV_A_WIDTH, D_A), CONV_A_WIDTH ** -0.5),
        "conv_a_b": nrm(ks[11], (DEPTH, D_A), 0.01),
        "norm_a_g": 1.0 + nrm(ks[12], (DEPTH, D_A), 0.02),
        "norm_a_b": nrm(ks[13], (DEPTH, D_A), 0.01),
        "w_a_out": nrm(ks[14], (DEPTH, D_A, D_MODEL), D_A ** -0.5 * DEEPNORM_BETA),
        "conv_b_w": nrm(ks[15], (DEPTH, CONV_B_WIDTH, D_XBC), CONV_B_WIDTH ** -0.5),
        "conv_b_b": nrm(ks[16], (DEPTH, D_XBC), 0.01),
        "dt_bias": dt0 + jnp.log(-jnp.expm1(-dt0)),
        "a_log": jnp.log(jax.random.uniform(ks[17], (DEPTH, SSM_HEADS), f32, 1.0, 16.0)),
        "d_skip": 1.0 + nrm(ks[18], (DEPTH, SSM_HEADS), 0.02),
        "gnorm_w": 1.0 + nrm(ks[19], (DEPTH, D_INNER), 0.02),
        "w_b_out": nrm(ks[20], (DEPTH, D_INNER, D_MODEL), D_INNER ** -0.5 * DEEPNORM_BETA),
        "w_out": nrm(ks[21], (DEPTH, D_MODEL, D_MODEL), D_MODEL ** -0.5 * DEEPNORM_BETA),
        "w_ple": nrm(ks[22], (DEPTH, PLE_DIM, D_MODEL), PLE_DIM ** -0.5 * DEEPNORM_BETA),
        "ln_g": 1.0 + nrm(ks[23], (DEPTH, D_MODEL), 0.02),
        "ln_b": nrm(ks[24], (DEPTH, D_MODEL), 0.01),
    }


def reference(x_prompt, x_sample, cache_conv_a, cache_conv_b, state_ssm, p_prompt, p_sample,
              w_in, b_in, conv_a_w, conv_a_b, norm_a_g, norm_a_b, w_a_out,
              conv_b_w, conv_b_b, dt_bias, a_log, d_skip, gnorm_w, w_b_out, w_out, w_ple, ln_g, ln_b):
    weights = (w_in, b_in, conv_a_w, conv_a_b, norm_a_g, norm_a_b, w_a_out,
               conv_b_w, conv_b_b, dt_bias, a_log, d_skip, gnorm_w, w_b_out, w_out, w_ple, ln_g, ln_b)
    bp = x_prompt.shape[0]
    zeros_a = jnp.zeros((DEPTH, bp, CONV_A_WIDTH - 1, D_A), x_prompt.dtype)
    zeros_b = jnp.zeros((DEPTH, bp, CONV_B_WIDTH - 1, D_XBC), x_prompt.dtype)
    zeros_h = jnp.zeros((DEPTH, bp, SSM_HEADS, SSM_HEAD_DIM, SSM_STATE), jnp.float32)
    y_prompt, conv_a_prompt, conv_b_prompt, ssm_prompt = run_trunk(x_prompt, p_prompt, zeros_a, zeros_b, zeros_h, weights)
    y_sample, conv_a_sample, conv_b_sample, ssm_sample = run_trunk(x_sample, p_sample, cache_conv_a, cache_conv_b, state_ssm, weights)
    return (y_prompt, y_sample, conv_a_prompt, conv_b_prompt, ssm_prompt, conv_a_sample, conv_b_sample, ssm_sample)
```

```python
import functools

import numpy as np
import jax
import jax.numpy as jnp
from jax import lax
from jax.experimental import pallas as pl
from jax.experimental.pallas import tpu as pltpu

F32 = jnp.float32
BF16 = jnp.bfloat16

D_MODEL = 1024
D_A = 1024
CONV_A_WIDTH = 31
D_INNER = 2048
SSM_HEAD_DIM = 64
SSM_HEADS = 32
SSM_GROUPS = 4
SSM_STATE = 128
CONV_B_WIDTH = 4
D_XBC = D_INNER + 2 * SSM_GROUPS * SSM_STATE
PLE_DIM = 256
CHUNK = 64
LN_EPS = 1e-5

LANES = 128
HEAD_REP = LANES // SSM_HEADS
GROUP_W = D_INNER // SSM_GROUPS
HALO_A = 32
HALO_B = 8
VMEM_LIMIT_BYTES = 60 * 1024 * 1024


def _sigmoid(v):
    return jax.nn.sigmoid(v)


def _silu(v):
    return v * jax.nn.sigmoid(v)


def _softplus(v):
    return jnp.maximum(v, 0.0) + jnp.log1p(jnp.exp(-jnp.abs(v)))


def _layer_norm(v, g, b):
    mu = jnp.mean(v, axis=-1, keepdims=True)
    c = v - mu
    var = jnp.mean(c * c, axis=-1, keepdims=True)
    return c * lax.rsqrt(var + LN_EPS) * g + b


def _dot(a, b):
    return jnp.dot(a, b, preferred_element_type=F32)


def _split_residuals(v):
    r1 = v - v.astype(BF16).astype(F32)
    r2 = r1 - r1.astype(BF16).astype(F32)
    return v, r1, r2


def _layer_kernel(*refs, nseq, lt, nt, has_cache, alpha):
    it = iter(refs)
    x_ref, p_ref = next(it), next(it)
    if has_cache:
        ca_in, cb_in, st_in = next(it), next(it), next(it)
    (wa_ref, ba_ref, wzx_ref, bzx_ref, wdt_ref, bdt_ref, dtb_ref, alog_ref, wg_ref, bg_ref,
     caw_ref, cab_ref, nag_ref, nab_ref, waout_ref, cbw_ref, cbb_ref, dskip_ref, gnw_ref,
     wbout_ref, wout_ref, wple_ref, lng_ref, lnb_ref, tri_ref, exp_ref) = [next(it) for _ in range(26)]
    y_ref, ca_ref, cb_ref, st_ref = next(it), next(it), next(it), next(it)
    ubuf, xbuf, xc_ref, dt_ref, yb_ref, z_ref, oa_ref, ht_ref, at_ref = [next(it) for _ in range(9)]

    t = pl.program_id(1)
    s = pl.program_id(2)
    rows = nseq * lt
    lp = xc_ref.shape[1]
    nch = lp // CHUNK

    @pl.when(s == 0)
    def _dense_in():
        @pl.when(t == 0)
        def _init():
            ubuf[:, 0:HALO_A, :] = jnp.zeros((nseq, HALO_A, D_A), F32)
            xbuf[:, 0:HALO_B, :] = jnp.zeros((nseq, HALO_B, D_XBC), F32)
            if has_cache:
                ubuf[:, HALO_A - (CONV_A_WIDTH - 1):HALO_A, :] = ca_in[...]
                xbuf[:, HALO_B - (CONV_B_WIDTH - 1):HALO_B, :] = cb_in[...]

        xb = x_ref[...].reshape(rows, D_MODEL).astype(BF16)

        pa = _dot(xb, wa_ref[...]) + ba_ref[...]
        u = pa[:, 0:D_A] * _sigmoid(pa[:, D_A:2 * D_A])
        ubuf[:, HALO_A:HALO_A + lt, :] = u.reshape(nseq, lt, D_A)
        acc = jnp.zeros((nseq, lt, D_A), F32) + cab_ref[...]
        for k in range(CONV_A_WIDTH):
            off = HALO_A - (CONV_A_WIDTH - 1) + k
            acc = acc + ubuf[:, off:off + lt, :] * caw_ref[k:k + 1, :]
        va = (_silu(_layer_norm(acc.reshape(rows, D_A), nag_ref[...], nab_ref[...]))
              * _silu(pa[:, 2 * D_A:3 * D_A]))
        oa_ref[...] = _dot(va.astype(BF16), waout_ref[...])

        pzx = _dot(xb, wzx_ref[...]) + bzx_ref[...]
        z_ref[...] = pzx[:, 0:D_INNER]
        xbuf[:, HALO_B:HALO_B + lt, :] = pzx[:, D_INNER:D_INNER + D_XBC].reshape(nseq, lt, D_XBC)
        accb = jnp.zeros((nseq, lt, D_XBC), F32) + cbb_ref[...]
        for k in range(CONV_B_WIDTH):
            off = HALO_B - (CONV_B_WIDTH - 1) + k
            accb = accb + xbuf[:, off:off + lt, :] * cbw_ref[k:k + 1, :]
        xc = _silu(accb)
        dt4 = _softplus(_dot(xb, wdt_ref[...]) + bdt_ref[...] + dtb_ref[...])
        if lp == lt:
            xc_ref[...] = xc
            dt_ref[...] = dt4.reshape(nseq, lt, LANES)
        else:
            xc_ref[:, 0:lt, :] = xc
            xc_ref[:, lt:lp, :] = jnp.zeros((nseq, lp - lt, D_XBC), F32)
            dt_ref[:, 0:lt, :] = dt4.reshape(nseq, lt, LANES)
            dt_ref[:, lt:lp, :] = jnp.zeros((nseq, lp - lt, LANES), F32)

    @pl.when(t == 0)
    def _init_state():
        if has_cache:
            ht_ref[...] = st_in[0].T
        else:
            ht_ref[...] = jnp.zeros(ht_ref.shape, F32)

    a_neg = -jnp.exp(alog_ref[...])
    lane = lax.broadcasted_iota(jnp.int32, (1, LANES), 1)
    lane_lo = lane < CHUNK
    row_l = lax.broadcasted_iota(jnp.int32, (CHUNK, LANES), 0)
    col_s = lax.broadcasted_iota(jnp.int32, (CHUNK, LANES), 1) % CHUNK
    causal2 = row_l >= col_s
    bd_r = lax.broadcasted_iota(jnp.int32, (2 * CHUNK, LANES), 0) >= CHUNK
    bd_c = lax.broadcasted_iota(jnp.int32, (2 * CHUNK, LANES), 1) >= SSM_HEAD_DIM
    blockdiag = bd_r == bd_c
    lane4 = lax.broadcasted_iota(jnp.int32, (CHUNK, LANES), 1)
    tri3 = tri_ref[...]
    expand = exp_ref[...]

    def _expand_heads(v4):
        v, r1, r2 = _split_residuals(v4)
        kv = jnp.where(lane4 < SSM_HEADS, v, jnp.where(lane4 < 2 * SSM_HEADS, r1,
                       jnp.where(lane4 < 3 * SSM_HEADS, r2, 0.0)))
        return _dot(kv.astype(BF16), expand)

    def _chunk(c, carry):
        rs = pl.ds(pl.multiple_of(c * CHUNK, CHUNK), CHUNK)
        dtc = dt_ref[s, rs, :]
        a4 = dtc * a_neg
        v, r1, r2 = _split_residuals(a4)
        stack = jnp.concatenate([v.astype(BF16), r1.astype(BF16), r2.astype(BF16)], axis=0)
        acum = _dot(tri3, stack)
        at_ref[...] = jnp.concatenate([acum, acum], axis=0).T
        last = acum[CHUNK - 1:CHUNK, :]
        expa = jnp.exp(acum)
        acum_e = _expand_heads(acum)
        dt_e = _expand_heads(dtc)
        dsd_e = _expand_heads(dtc * jnp.exp(last - acum))
        expa_e = _expand_heads(expa)
        xs = xc_ref[s, rs, 0:D_INNER]
        xdt = xs * dt_e
        xst = (xs * dsd_e).astype(BF16)
        for g in range(SSM_GROUPS):
            gs = slice(g * GROUP_W, (g + 1) * GROUP_W)
            bg = xc_ref[s, rs, D_INNER + g * SSM_STATE:D_INNER + (g + 1) * SSM_STATE].astype(BF16)
            cg = xc_ref[s, rs, D_INNER + (SSM_GROUPS + g) * SSM_STATE:
                        D_INNER + (SSM_GROUPS + g + 1) * SSM_STATE].astype(BF16)
            b2 = jnp.concatenate([bg, bg], axis=0)
            cb2 = lax.dot_general(cg, b2, (((1,), (1,)), ((), ())), preferred_element_type=F32)
            htg = ht_ref[:, gs]
            yoff = _dot(cg, htg.astype(BF16))
            ydiag = []
            for j in range(GROUP_W // LANES):
                pj = g * (GROUP_W // LANES) + j
                cs = slice(pj * LANES, (pj + 1) * LANES)
                rowv = jnp.where(lane_lo, at_ref[2 * pj:2 * pj + 1, :], at_ref[2 * pj + 1:2 * pj + 2, :])
                seg = acum_e[:, cs] - rowv
                lm = jnp.exp(jnp.where(causal2, seg, -jnp.inf))
                mp = (cb2 * lm).astype(BF16)
                xp = xdt[:, cs]
                xbd = jnp.where(blockdiag, jnp.concatenate([xp, xp], axis=0), 0.0).astype(BF16)
                ydiag.append(_dot(mp, xbd))
            yg = (jnp.concatenate(ydiag, axis=1) + expa_e[:, gs] * yoff
                  + dskip_ref[:, gs] * xs[:, gs])
            yb_ref[s, rs, gs] = yg
            upd = lax.dot_general(bg, xst[:, gs], (((0,), (0,)), ((), ())), preferred_element_type=F32)
            ht_ref[:, gs] = htg * expa_e[CHUNK - 1:CHUNK, gs] + upd
        return carry

    lax.fori_loop(0, nch, _chunk, 0)

    @pl.when(t == nt - 1)
    def _final_state():
        st_ref[0] = ht_ref[...].T

    @pl.when(s == nseq - 1)
    def _dense_out():
        x = x_ref[...].reshape(rows, D_MODEL)
        xb = x.astype(BF16)
        yv = yb_ref[:, 0:lt, :].reshape(rows, D_INNER)
        gz = (yv * _silu(z_ref[...])).reshape(rows, SSM_GROUPS, GROUP_W)
        gz = gz * lax.rsqrt(jnp.mean(gz * gz, axis=-1, keepdims=True) + LN_EPS)
        gn = gz.reshape(rows, D_INNER) * gnw_ref[...]
        out_b = _dot(gn.astype(BF16), wbout_ref[...])

        pg = _dot(xb, wg_ref[...]) + bg_ref[...]
        merged = (_sigmoid(pg[:, 0:D_MODEL]) * oa_ref[...]
                  + _sigmoid(pg[:, D_MODEL:2 * D_MODEL]) * out_b)
        pb = p_ref[...].reshape(rows, PLE_DIM).astype(BF16)
        ple = _sigmoid(pg[:, 2 * D_MODEL:3 * D_MODEL]) * _dot(pb, wple_ref[...])
        r = alpha * x + _dot(merged.astype(BF16), wout_ref[...]) + ple
        y_ref[...] = _layer_norm(r, lng_ref[...], lnb_ref[...]).reshape(nseq, lt, D_MODEL)

        @pl.when(t == nt - 1)
        def _finish():
            ca_ref[...] = ubuf[:, lt + HALO_A - (CONV_A_WIDTH - 1):lt + HALO_A, :]
            cb_ref[...] = xbuf[:, lt + HALO_B - (CONV_B_WIDTH - 1):lt + HALO_B, :]

        if nt > 1:
            ubuf[:, 0:HALO_A, :] = ubuf[:, lt:lt + HALO_A, :]
            xbuf[:, 0:HALO_B, :] = xbuf[:, lt:lt + HALO_B, :]


def _head_constants():
    k = np.arange(LANES)
    c = np.arange(D_INNER)
    expand = ((k[:, None] < 3 * SSM_HEADS) & ((k[:, None] % SSM_HEADS) == (c[None, :] // SSM_HEAD_DIM)))
    l = np.arange(CHUNK)
    kk = np.arange(3 * CHUNK)
    tri3 = (kk[None, :] % CHUNK) <= l[:, None]
    return jnp.asarray(tri3, BF16), jnp.asarray(expand, BF16)


def _layer_call(x, p, caches, lw, *, nseq, lt, alpha):
    nb, seq, _ = x.shape
    nt = seq // lt
    assert nseq == 1 or nt == 1
    lp = -(-lt // CHUNK) * CHUNK
    rows = nseq * lt
    has_cache = caches is not None
    tri3, expand = _head_constants()
    consts = list(lw) + [tri3, expand]

    def tile_spec(width):
        return pl.BlockSpec((nseq, lt, width), lambda b, t, s: (b, t, 0))

    def tile_seq_spec(r, width):
        return pl.BlockSpec((nseq, r, width), lambda b, t, s: (b, 0, 0))

    def state_spec():
        return pl.BlockSpec((1, D_INNER, SSM_STATE), lambda b, t, s: (b * nseq + s, 0, 0))

    def whole(a):
        return pl.BlockSpec(a.shape, lambda b, t, s: (0,) * a.ndim)

    in_specs = [tile_spec(D_MODEL), tile_spec(PLE_DIM)]
    args = [x, p]
    if has_cache:
        in_specs += [tile_seq_spec(CONV_A_WIDTH - 1, D_A), tile_seq_spec(CONV_B_WIDTH - 1, D_XBC), state_spec()]
        args += list(caches)
    in_specs += [whole(a) for a in consts]
    args += consts
    out_shape = (jax.ShapeDtypeStruct((nb, seq, D_MODEL), F32),
                 jax.ShapeDtypeStruct((nb, CONV_A_WIDTH - 1, D_A), F32),
                 jax.ShapeDtypeStruct((nb, CONV_B_WIDTH - 1, D_XBC), F32),
                 jax.ShapeDtypeStruct((nb, D_INNER, SSM_STATE), F32))
    out_specs = (tile_spec(D_MODEL), tile_seq_spec(CONV_A_WIDTH - 1, D_A),
                 tile_seq_spec(CONV_B_WIDTH - 1, D_XBC), state_spec())
    scratch = [pltpu.VMEM((nseq, HALO_A + lt, D_A), F32),
               pltpu.VMEM((nseq, HALO_B + lt, D_XBC), F32),
               pltpu.VMEM((nseq, lp, D_XBC), F32),
               pltpu.VMEM((nseq, lp, LANES), F32),
               pltpu.VMEM((nseq, lp, D_INNER), F32),
               pltpu.VMEM((rows, D_INNER), F32),
               pltpu.VMEM((rows, D_MODEL), F32),
               pltpu.VMEM((SSM_STATE, D_INNER), F32),
               pltpu.VMEM((LANES, LANES), F32)]
    kern = functools.partial(_layer_kernel, nseq=nseq, lt=lt, nt=nt, has_cache=has_cache, alpha=alpha)
    return pl.pallas_call(
        kern,
        grid=(nb // nseq, nt, nseq),
        in_specs=in_specs,
        out_specs=out_specs,
        out_shape=out_shape,
        scratch_shapes=scratch,
        compiler_params=pltpu.CompilerParams(
            dimension_semantics=("arbitrary", "arbitrary", "arbitrary"),
            vmem_limit_bytes=VMEM_LIMIT_BYTES),
    )(*args)


def _prep_layer_weights(i, w_in, b_in, conv_a_w, conv_a_b, norm_a_g, norm_a_b, w_a_out, conv_b_w, conv_b_b,
                        dt_bias, a_log, d_skip, gnorm_w, w_b_out, w_out, w_ple, ln_g, ln_b):
    o_z = 3 * D_A
    o_dt = o_z + D_INNER + D_XBC
    o_g = o_dt + SSM_HEADS
    wi, bi = w_in[i], b_in[i]
    row = lambda v: v.reshape(1, -1)
    rep = lambda v: jnp.tile(v, HEAD_REP).reshape(1, LANES)
    return [
        wi[:, 0:o_z].astype(BF16), row(bi[0:o_z]),
        wi[:, o_z:o_dt].astype(BF16), row(bi[o_z:o_dt]),
        jnp.tile(wi[:, o_dt:o_g], (1, HEAD_REP)).astype(BF16), rep(bi[o_dt:o_g]),
        rep(dt_bias[i]), rep(a_log[i]),
        wi[:, o_g:].astype(BF16), row(bi[o_g:]),
        conv_a_w[i], row(conv_a_b[i]), row(norm_a_g[i]), row(norm_a_b[i]), w_a_out[i].astype(BF16),
        conv_b_w[i], row(conv_b_b[i]), row(jnp.repeat(d_skip[i], SSM_HEAD_DIM)), row(gnorm_w[i]),
        w_b_out[i].astype(BF16), w_out[i].astype(BF16), w_ple[i].astype(BF16), row(ln_g[i]), row(ln_b[i]),
    ]


def _run_trunk(x, p, caches, weights, *, nseq, lt):
    depth = weights[0].shape[0]
    alpha = float((2 * depth) ** 0.25)
    new_a, new_b, new_h = [], [], []
    for i in range(depth):
        lw = _prep_layer_weights(i, *weights)
        if caches is None:
            c = None
        else:
            nb = x.shape[0]
            c = (caches[0][i], caches[1][i], caches[2][i].reshape(nb, D_INNER, SSM_STATE))
        x, ca, cb, st = _layer_call(x, p[i], c, lw, nseq=nseq, lt=lt, alpha=alpha)
        new_a.append(ca)
        new_b.append(cb)
        new_h.append(st.reshape(st.shape[0], SSM_HEADS, SSM_HEAD_DIM, SSM_STATE))
    return x, jnp.stack(new_a), jnp.stack(new_b), jnp.stack(new_h)


def kernel(x_prompt, x_sample, cache_conv_a, cache_conv_b, state_ssm, p_prompt, p_sample, w_in, b_in, conv_a_w, conv_a_b, norm_a_g, norm_a_b, w_a_out, conv_b_w, conv_b_b, dt_bias, a_log, d_skip, gnorm_w, w_b_out, w_out, w_ple, ln_g, ln_b):
    weights = (w_in, b_in, conv_a_w, conv_a_b, norm_a_g, norm_a_b, w_a_out, conv_b_w, conv_b_b,
               dt_bias, a_log, d_skip, gnorm_w, w_b_out, w_out, w_ple, ln_g, ln_b)
    y_p, ca_p, cb_p, h_p = _run_trunk(x_prompt, p_prompt, None, weights, nseq=1, lt=256)
    y_s, ca_s, cb_s, h_s = _run_trunk(x_sample, p_sample, (cache_conv_a, cache_conv_b, state_ssm), weights,
                                      nseq=4, lt=x_sample.shape[1])
    return (y_p, y_s, ca_p, cb_p, h_p, ca_s, cb_s, h_s)
```

```python
import functools

import numpy as np
import jax
import jax.numpy as jnp
from jax import lax
from jax.experimental import pallas as pl
from jax.experimental.pallas import tpu as pltpu

F32 = jnp.float32
BF16 = jnp.bfloat16

D_MODEL = 1024
D_A = 1024
CONV_A_WIDTH = 31
D_INNER = 2048
SSM_HEAD_DIM = 64
SSM_HEADS = 32
SSM_GROUPS = 4
SSM_STATE = 128
CONV_B_WIDTH = 4
D_XBC = D_INNER + 2 * SSM_GROUPS * SSM_STATE
PLE_DIM = 256
CHUNK = 64
LN_EPS = 1e-5

LANES = 128
HEAD_REP = LANES // SSM_HEADS
GROUP_W = D_INNER // SSM_GROUPS
HALO_A = 32
HALO_B = 8
VMEM_LIMIT_BYTES = 60 * 1024 * 1024


def _sigmoid(v):
    return jax.nn.sigmoid(v)


def _silu(v):
    return v * jax.nn.sigmoid(v)


def _softplus(v):
    return jnp.maximum(v, 0.0) + jnp.log1p(jnp.exp(-jnp.abs(v)))


def _layer_norm(v, g, b):
    mu = jnp.mean(v, axis=-1, keepdims=True)
    c = v - mu
    var = jnp.mean(c * c, axis=-1, keepdims=True)
    return c * lax.rsqrt(var + LN_EPS) * g + b


def _dot(a, b):
    return jnp.dot(a, b, preferred_element_type=F32)


def _split_residuals(v):
    r1 = v - v.astype(BF16).astype(F32)
    r2 = r1 - r1.astype(BF16).astype(F32)
    return v, r1, r2


def _layer_kernel(*refs, nseq, lt, nt, has_cache, alpha):
    it = iter(refs)
    x_ref, p_ref = next(it), next(it)
    if has_cache:
        ca_in, cb_in, st_in = next(it), next(it), next(it)
    (wa_ref, ba_ref, wzx_ref, bzx_ref, wdt_ref, bdt_ref, dtb_ref, alog_ref, wg_ref, bg_ref,
     caw_ref, cab_ref, nag_ref, nab_ref, waout_ref, cbw_ref, cbb_ref, dskip_ref, gnw_ref,
     wbout_ref, wout_ref, wple_ref, lng_ref, lnb_ref, tri_ref, exp_ref) = [next(it) for _ in range(26)]
    y_ref, ca_ref, cb_ref, st_ref = next(it), next(it), next(it), next(it)
    ubuf, xbuf, cv_ref, xc_ref, dt_ref, yb_ref, z_ref, oa_ref, ht_ref, at_ref = [next(it) for _ in range(10)]

    t = pl.program_id(1)
    s = pl.program_id(2)
    rows = nseq * lt
    lp = xc_ref.shape[1]
    nch = lp // CHUNK
    hist_a = CONV_A_WIDTH - 1
    hist_b = CONV_B_WIDTH - 1

    def _dwconv(buf, w_ref, b_ref, halo, width, store):
        rb = min(lt, CHUNK)
        for q in range(nseq):
            for lb in range(buf.shape[1]):
                ls = slice(lb * LANES, (lb + 1) * LANES)
                for r0 in range(0, lt, rb):
                    acc = jnp.zeros((rb, LANES), F32) + b_ref[:, ls]
                    for k in range(width):
                        off = halo - (width - 1) + k + r0
                        acc = acc + buf[q, lb, off:off + rb, :] * w_ref[k:k + 1, ls]
                    store(q, r0, rb, ls, acc)

    @pl.when(s == 0)
    def _dense_in():
        @pl.when(t == 0)
        def _init():
            ubuf[:, :, 0:HALO_A, :] = jnp.zeros((nseq, D_A // LANES, HALO_A, LANES), F32)
            xbuf[:, :, 0:HALO_B, :] = jnp.zeros((nseq, D_XBC // LANES, HALO_B, LANES), F32)
            if has_cache:
                for lb in range(D_A // LANES):
                    ubuf[:, lb, HALO_A - hist_a:HALO_A, :] = ca_in[:, :, lb * LANES:(lb + 1) * LANES]
                for lb in range(D_XBC // LANES):
                    xbuf[:, lb, HALO_B - hist_b:HALO_B, :] = cb_in[:, :, lb * LANES:(lb + 1) * LANES]

        xb = x_ref[...].reshape(rows, D_MODEL).astype(BF16)

        pa = _dot(xb, wa_ref[...]) + ba_ref[...]
        u = pa[:, 0:D_A] * _sigmoid(pa[:, D_A:2 * D_A])
        for lb in range(D_A // LANES):
            ubuf[:, lb, HALO_A:HALO_A + lt, :] = u[:, lb * LANES:(lb + 1) * LANES].reshape(nseq, lt, LANES)

        def _store_a(q, r0, rb, ls, acc):
            cv_ref[q * lt + r0:q * lt + r0 + rb, ls] = acc

        _dwconv(ubuf, caw_ref, cab_ref, HALO_A, CONV_A_WIDTH, _store_a)
        va = (_silu(_layer_norm(cv_ref[...], nag_ref[...], nab_ref[...]))
              * _silu(pa[:, 2 * D_A:3 * D_A]))
        oa_ref[...] = _dot(va.astype(BF16), waout_ref[...])

        pzx = _dot(xb, wzx_ref[...]) + bzx_ref[...]
        z_ref[...] = pzx[:, 0:D_INNER]
        for lb in range(D_XBC // LANES):
            xbuf[:, lb, HALO_B:HALO_B + lt, :] = (
                pzx[:, D_INNER + lb * LANES:D_INNER + (lb + 1) * LANES].reshape(nseq, lt, LANES))

        def _store_b(q, r0, rb, ls, acc):
            xc_ref[q, r0:r0 + rb, ls] = _silu(acc)

        _dwconv(xbuf, cbw_ref, cbb_ref, HALO_B, CONV_B_WIDTH, _store_b)
        dt4 = _softplus(_dot(xb, wdt_ref[...]) + bdt_ref[...] + dtb_ref[...])
        dt_ref[:, 0:lt, :] = dt4.reshape(nseq, lt, LANES)
        if lp != lt:
            xc_ref[:, lt:lp, :] = jnp.zeros((nseq, lp - lt, D_XBC), F32)
            dt_ref[:, lt:lp, :] = jnp.zeros((nseq, lp - lt, LANES), F32)

    @pl.when(t == 0)
    def _init_state():
        if has_cache:
            ht_ref[...] = st_in[0].T
        else:
            ht_ref[...] = jnp.zeros(ht_ref.shape, F32)

    a_neg = -jnp.exp(alog_ref[...])
    lane = lax.broadcasted_iota(jnp.int32, (1, LANES), 1)
    lane_lo = lane < CHUNK
    row_l = lax.broadcasted_iota(jnp.int32, (CHUNK, LANES), 0)
    col_s = lax.broadcasted_iota(jnp.int32, (CHUNK, LANES), 1) % CHUNK
    causal2 = row_l >= col_s
    bd_r = lax.broadcasted_iota(jnp.int32, (2 * CHUNK, LANES), 0) >= CHUNK
    bd_c = lax.broadcasted_iota(jnp.int32, (2 * CHUNK, LANES), 1) >= SSM_HEAD_DIM
    blockdiag = bd_r == bd_c
    lane4 = lax.broadcasted_iota(jnp.int32, (CHUNK, LANES), 1)
    tri3 = tri_ref[...]
    expand = exp_ref[...]

    def _expand_heads(v4):
        v, r1, r2 = _split_residuals(v4)
        kv = jnp.where(lane4 < SSM_HEADS, v, jnp.where(lane4 < 2 * SSM_HEADS, r1,
                       jnp.where(lane4 < 3 * SSM_HEADS, r2, 0.0)))
        return _dot(kv.astype(BF16), expand)

    def _chunk(c, carry):
        rs = pl.ds(pl.multiple_of(c * CHUNK, CHUNK), CHUNK)
        dtc = dt_ref[s, rs, :]
        a4 = dtc * a_neg
        v, r1, r2 = _split_residuals(a4)
        stack = jnp.concatenate([v.astype(BF16), r1.astype(BF16), r2.astype(BF16)], axis=0)
        acum = _dot(tri3, stack)
        at_ref[...] = jnp.concatenate([acum, acum], axis=0).T
        last = acum[CHUNK - 1:CHUNK, :]
        expa = jnp.exp(acum)
        acum_e = _expand_heads(acum)
        dt_e = _expand_heads(dtc)
        dsd_e = _expand_heads(dtc * jnp.exp(last - acum))
        expa_e = _expand_heads(expa)
        xs = xc_ref[s, rs, 0:D_INNER]
        xdt = xs * dt_e
        xst = (xs * dsd_e).astype(BF16)
        for g in range(SSM_GROUPS):
            gs = slice(g * GROUP_W, (g + 1) * GROUP_W)
            bg = xc_ref[s, rs, D_INNER + g * SSM_STATE:D_INNER + (g + 1) * SSM_STATE].astype(BF16)
            cg = xc_ref[s, rs, D_INNER + (SSM_GROUPS + g) * SSM_STATE:
                        D_INNER + (SSM_GROUPS + g + 1) * SSM_STATE].astype(BF16)
            b2 = jnp.concatenate([bg, bg], axis=0)
            cb2 = lax.dot_general(cg, b2, (((1,), (1,)), ((), ())), preferred_element_type=F32)
            htg = ht_ref[:, gs]
            yoff = _dot(cg, htg.astype(BF16))
            ydiag = []
            for j in range(GROUP_W // LANES):
                pj = g * (GROUP_W // LANES) + j
                cs = slice(pj * LANES, (pj + 1) * LANES)
                rowv = jnp.where(lane_lo, at_ref[2 * pj:2 * pj + 1, :], at_ref[2 * pj + 1:2 * pj + 2, :])
                seg = acum_e[:, cs] - rowv
                lm = jnp.exp(jnp.where(causal2, seg, -jnp.inf))
                mp = (cb2 * lm).astype(BF16)
                xp = xdt[:, cs]
                xbd = jnp.where(blockdiag, jnp.concatenate([xp, xp], axis=0), 0.0).astype(BF16)
                ydiag.append(_dot(mp, xbd))
            yg = (jnp.concatenate(ydiag, axis=1) + expa_e[:, gs] * yoff
                  + dskip_ref[:, gs] * xs[:, gs])
            yb_ref[s, rs, gs] = yg
            upd = lax.dot_general(bg, xst[:, gs], (((0,), (0,)), ((), ())), preferred_element_type=F32)
            ht_ref[:, gs] = htg * expa_e[CHUNK - 1:CHUNK, gs] + upd
        return carry

    lax.fori_loop(0, nch, _chunk, 0)

    @pl.when(t == nt - 1)
    def _final_state():
        st_ref[0] = ht_ref[...].T

    @pl.when(s == nseq - 1)
    def _dense_out():
        x = x_ref[...].reshape(rows, D_MODEL)
        xb = x.astype(BF16)
        yv = yb_ref[:, 0:lt, :].reshape(rows, D_INNER)
        gz = (yv * _silu(z_ref[...])).reshape(rows, SSM_GROUPS, GROUP_W)
        gz = gz * lax.rsqrt(jnp.mean(gz * gz, axis=-1, keepdims=True) + LN_EPS)
        gn = gz.reshape(rows, D_INNER) * gnw_ref[...]
        out_b = _dot(gn.astype(BF16), wbout_ref[...])

        pg = _dot(xb, wg_ref[...]) + bg_ref[...]
        merged = (_sigmoid(pg[:, 0:D_MODEL]) * oa_ref[...]
                  + _sigmoid(pg[:, D_MODEL:2 * D_MODEL]) * out_b)
        pb = p_ref[...].reshape(rows, PLE_DIM).astype(BF16)
        ple = _sigmoid(pg[:, 2 * D_MODEL:3 * D_MODEL]) * _dot(pb, wple_ref[...])
        r = alpha * x + _dot(merged.astype(BF16), wout_ref[...]) + ple
        y_ref[...] = _layer_norm(r, lng_ref[...], lnb_ref[...]).reshape(nseq, lt, D_MODEL)

        @pl.when(t == nt - 1)
        def _finish():
            for lb in range(D_A // LANES):
                ca_ref[:, :, lb * LANES:(lb + 1) * LANES] = ubuf[:, lb, lt + HALO_A - hist_a:lt + HALO_A, :]
            for lb in range(D_XBC // LANES):
                cb_ref[:, :, lb * LANES:(lb + 1) * LANES] = xbuf[:, lb, lt + HALO_B - hist_b:lt + HALO_B, :]

        if nt > 1:
            ubuf[:, :, 0:HALO_A, :] = ubuf[:, :, lt:lt + HALO_A, :]
            xbuf[:, :, 0:HALO_B, :] = xbuf[:, :, lt:lt + HALO_B, :]


def _head_constants():
    k = np.arange(LANES)
    c = np.arange(D_INNER)
    expand = ((k[:, None] < 3 * SSM_HEADS) & ((k[:, None] % SSM_HEADS) == (c[None, :] // SSM_HEAD_DIM)))
    l = np.arange(CHUNK)
    kk = np.arange(3 * CHUNK)
    tri3 = (kk[None, :] % CHUNK) <= l[:, None]
    return jnp.asarray(tri3, BF16), jnp.asarray(expand, BF16)


def _layer_call(x, p, caches, lw, *, nseq, lt, alpha):
    nb, seq, _ = x.shape
    nt = seq // lt
    assert nseq == 1 or nt == 1
    lp = -(-lt // CHUNK) * CHUNK
    rows = nseq * lt
    has_cache = caches is not None
    tri3, expand = _head_constants()
    consts = list(lw) + [tri3, expand]

    def tile_spec(width):
        return pl.BlockSpec((nseq, lt, width), lambda b, t, s: (b, t, 0))

    def tile_seq_spec(r, width):
        return pl.BlockSpec((nseq, r, width), lambda b, t, s: (b, 0, 0))

    def state_spec():
        return pl.BlockSpec((1, D_INNER, SSM_STATE), lambda b, t, s: (b * nseq + s, 0, 0))

    def whole(a):
        return pl.BlockSpec(a.shape, lambda b, t, s: (0,) * a.ndim)

    in_specs = [tile_spec(D_MODEL), tile_spec(PLE_DIM)]
    args = [x, p]
    if has_cache:
        in_specs += [tile_seq_spec(CONV_A_WIDTH - 1, D_A), tile_seq_spec(CONV_B_WIDTH - 1, D_XBC), state_spec()]
        args += list(caches)
    in_specs += [whole(a) for a in consts]
    args += consts
    out_shape = (jax.ShapeDtypeStruct((nb, seq, D_MODEL), F32),
                 jax.ShapeDtypeStruct((nb, CONV_A_WIDTH - 1, D_A), F32),
                 jax.ShapeDtypeStruct((nb, CONV_B_WIDTH - 1, D_XBC), F32),
                 jax.ShapeDtypeStruct((nb, D_INNER, SSM_STATE), F32))
    out_specs = (tile_spec(D_MODEL), tile_seq_spec(CONV_A_WIDTH - 1, D_A),
                 tile_seq_spec(CONV_B_WIDTH - 1, D_XBC), state_spec())
    scratch = [pltpu.VMEM((nseq, D_A // LANES, HALO_A + lt, LANES), F32),
               pltpu.VMEM((nseq, D_XBC // LANES, HALO_B + lt, LANES), F32),
               pltpu.VMEM((rows, D_A), F32),
               pltpu.VMEM((nseq, lp, D_XBC), F32),
               pltpu.VMEM((nseq, lp, LANES), F32),
               pltpu.VMEM((nseq, lp, D_INNER), F32),
               pltpu.VMEM((rows, D_INNER), F32),
               pltpu.VMEM((rows, D_MODEL), F32),
               pltpu.VMEM((SSM_STATE, D_INNER), F32),
               pltpu.VMEM((LANES, LANES), F32)]
    kern = functools.partial(_layer_kernel, nseq=nseq, lt=lt, nt=nt, has_cache=has_cache, alpha=alpha)
    return pl.pallas_call(
        kern,
        grid=(nb // nseq, nt, nseq),
        in_specs=in_specs,
        out_specs=out_specs,
        out_shape=out_shape,
        scratch_shapes=scratch,
        compiler_params=pltpu.CompilerParams(
            dimension_semantics=("arbitrary", "arbitrary", "arbitrary"),
            vmem_limit_bytes=VMEM_LIMIT_BYTES),
    )(*args)


def _prep_layer_weights(i, w_in, b_in, conv_a_w, conv_a_b, norm_a_g, norm_a_b, w_a_out, conv_b_w, conv_b_b,
                        dt_bias, a_log, d_skip, gnorm_w, w_b_out, w_out, w_ple, ln_g, ln_b):
    o_z = 3 * D_A
    o_dt = o_z + D_INNER + D_XBC
    o_g = o_dt + SSM_HEADS
    wi, bi = w_in[i], b_in[i]
    row = lambda v: v.reshape(1, -1)
    rep = lambda v: jnp.tile(v, HEAD_REP).reshape(1, LANES)
    return [
        wi[:, 0:o_z].astype(BF16), row(bi[0:o_z]),
        wi[:, o_z:o_dt].astype(BF16), row(bi[o_z:o_dt]),
        jnp.tile(wi[:, o_dt:o_g], (1, HEAD_REP)).astype(BF16), rep(bi[o_dt:o_g]),
        rep(dt_bias[i]), rep(a_log[i]),
        wi[:, o_g:].astype(BF16), row(bi[o_g:]),
        conv_a_w[i], row(conv_a_b[i]), row(norm_a_g[i]), row(norm_a_b[i]), w_a_out[i].astype(BF16),
        conv_b_w[i], row(conv_b_b[i]), row(jnp.repeat(d_skip[i], SSM_HEAD_DIM)), row(gnorm_w[i]),
        w_b_out[i].astype(BF16), w_out[i].astype(BF16), w_ple[i].astype(BF16), row(ln_g[i]), row(ln_b[i]),
    ]


def _run_trunk(x, p, caches, weights, *, nseq, lt):
    depth = weights[0].shape[0]
    alpha = float((2 * depth) ** 0.25)
    new_a, new_b, new_h = [], [], []
    for i in range(depth):
        lw = _prep_layer_weights(i, *weights)
        if caches is None:
            c = None
        else:
            nb = x.shape[0]
            c = (caches[0][i], caches[1][i], caches[2][i].reshape(nb, D_INNER, SSM_STATE))
        x, ca, cb, st = _layer_call(x, p[i], c, lw, nseq=nseq, lt=lt, alpha=alpha)
        new_a.append(ca)
        new_b.append(cb)
        new_h.append(st.reshape(st.shape[0], SSM_HEADS, SSM_HEAD_DIM, SSM_STATE))
    return x, jnp.stack(new_a), jnp.stack(new_b), jnp.stack(new_h)


def kernel(x_prompt, x_sample, cache_conv_a, cache_conv_b, state_ssm, p_prompt, p_sample, w_in, b_in, conv_a_w, conv_a_b, norm_a_g, norm_a_b, w_a_out, conv_b_w, conv_b_b, dt_bias, a_log, d_skip, gnorm_w, w_b_out, w_out, w_ple, ln_g, ln_b):
    weights = (w_in, b_in, conv_a_w, conv_a_b, norm_a_g, norm_a_b, w_a_out, conv_b_w, conv_b_b,
               dt_bias, a_log, d_skip, gnorm_w, w_b_out, w_out, w_ple, ln_g, ln_b)
    y_p, ca_p, cb_p, h_p = _run_trunk(x_prompt, p_prompt, None, weights, nseq=1, lt=256)
    y_s, ca_s, cb_s, h_s = _run_trunk(x_sample, p_sample, (cache_conv_a, cache_conv_b, state_ssm), weights,
                                      nseq=4, lt=x_sample.shape[1])
    return (y_p, y_s, ca_p, cb_p, h_p, ca_s, cb_s, h_s)
```

```python
import functools

import numpy as np
import jax
import jax.numpy as jnp
from jax import lax
from jax.experimental import pallas as pl
from jax.experimental.pallas import tpu as pltpu

F32 = jnp.float32
BF16 = jnp.bfloat16

D_MODEL = 1024
D_A = 1024
CONV_A_WIDTH = 31
D_INNER = 2048
SSM_HEAD_DIM = 64
SSM_HEADS = 32
SSM_GROUPS = 4
SSM_STATE = 128
CONV_B_WIDTH = 4
D_XBC = D_INNER + 2 * SSM_GROUPS * SSM_STATE
PLE_DIM = 256
CHUNK = 64
LN_EPS = 1e-5

LANES = 128
HEAD_REP = LANES // SSM_HEADS
GROUP_W = D_INNER // SSM_GROUPS
PAIRS_PER_GROUP = GROUP_W // LANES
HALO_A = 32
HALO_B = 8
COL_CHUNK = 512
VMEM_LIMIT_BYTES = 60 * 1024 * 1024


def _sigmoid(v):
    return jax.nn.sigmoid(v)


def _silu(v):
    return v * jax.nn.sigmoid(v)


def _softplus(v):
    return jnp.maximum(v, 0.0) + jnp.log1p(jnp.exp(-jnp.abs(v)))


def _layer_norm(v, g, b):
    mu = jnp.mean(v, axis=-1, keepdims=True)
    c = v - mu
    var = jnp.mean(c * c, axis=-1, keepdims=True)
    return c * lax.rsqrt(var + LN_EPS) * g + b


def _dot(a, b):
    return jnp.dot(a, b, preferred_element_type=F32)


def _w(ref, r0, r1, c0, c1):
    return pltpu.bitcast(ref[r0 // 2:r1 // 2, c0:c1], BF16)


def _split_residuals(v):
    r1 = v - v.astype(BF16).astype(F32)
    r2 = r1 - r1.astype(BF16).astype(F32)
    return v, r1, r2


def _run_interleaved(a_tasks, b_tasks):
    na, nb = len(a_tasks), len(b_tasks)
    ia = ib = 0
    while ia < na or ib < nb:
        if ib >= nb or (ia < na and ia * nb <= ib * na):
            a_tasks[ia]()
            ia += 1
        else:
            b_tasks[ib]()
            ib += 1


def _layer_kernel(*refs, nseq, lt, nt, has_cache, alpha):
    it = iter(refs)
    x_ref, p_ref = next(it), next(it)
    if has_cache:
        ca_in, cb_in, st_in = next(it), next(it), next(it)
    (wa_ref, ba_ref, wzx_ref, bzx_ref, wdt_ref, bdt_ref, dtb_ref, alog_ref, wg_ref, bg_ref,
     caw_ref, cab_ref, nag_ref, nab_ref, waout_ref, cbw_ref, cbb_ref, dskip_ref, gnw_ref,
     wbout_ref, wout_ref, wple_ref, lng_ref, lnb_ref, tri_ref, exp_ref) = [next(it) for _ in range(26)]
    y_ref, ca_ref, cb_ref, st_ref = next(it), next(it), next(it), next(it)
    (ubuf, xbuf, xb_ref, cv_ref, ag_ref, va_ref, xc_ref, dt_ref, sz_ref, sg_ref, oa_ref,
     ht_ref, at_ref) = [next(it) for _ in range(13)]

    t = pl.program_id(1)
    s = pl.program_id(2)
    rows = nseq * lt
    lp = xc_ref.shape[1]
    nch = lp // CHUNK
    hist_a = CONV_A_WIDTH - 1
    hist_b = CONV_B_WIDTH - 1
    conv_rb = min(lt, CHUNK)

    def _dwconv_block(buf, w_ref, b_ref, halo, width, q, lb, r0):
        ls = slice(lb * LANES, (lb + 1) * LANES)
        acc = jnp.zeros((conv_rb, LANES), F32) + b_ref[:, ls]
        for k in range(width):
            off = halo - (width - 1) + k + r0
            acc = acc + buf[q, lb, off:off + conv_rb, :] * w_ref[k:k + 1, ls]
        return acc

    @pl.when(s == 0)
    def _dense_in():
        @pl.when(t == 0)
        def _init():
            ubuf[:, :, 0:HALO_A, :] = jnp.zeros((nseq, D_A // LANES, HALO_A, LANES), F32)
            xbuf[:, :, 0:HALO_B, :] = jnp.zeros((nseq, D_XBC // LANES, HALO_B, LANES), F32)
            if has_cache:
                for lb in range(D_A // LANES):
                    ubuf[:, lb, HALO_A - hist_a:HALO_A, :] = ca_in[:, :, lb * LANES:(lb + 1) * LANES]
                for lb in range(D_XBC // LANES):
                    xbuf[:, lb, HALO_B - hist_b:HALO_B, :] = cb_in[:, :, lb * LANES:(lb + 1) * LANES]

        xb_ref[...] = x_ref[...].reshape(rows, D_MODEL).astype(BF16)

        def _proj(w_ref, b_ref, c0, width=COL_CHUNK):
            return _dot(xb_ref[...], _w(w_ref, 0, D_MODEL, c0, c0 + width)) + b_ref[:, c0:c0 + width]

        for c0 in range(0, D_A, COL_CHUNK):
            u = _proj(wa_ref, ba_ref, c0) * _sigmoid(_proj(wa_ref, ba_ref, D_A + c0))
            for j in range(COL_CHUNK // LANES):
                ubuf[:, c0 // LANES + j, HALO_A:HALO_A + lt, :] = (
                    u[:, j * LANES:(j + 1) * LANES].reshape(nseq, lt, LANES))
            ag_ref[:, c0:c0 + COL_CHUNK] = _silu(_proj(wa_ref, ba_ref, 2 * D_A + c0))

        def _conv_a_task(q, lb, r0):
            def run():
                cv_ref[q * lt + r0:q * lt + r0 + conv_rb, lb * LANES:(lb + 1) * LANES] = _dwconv_block(
                    ubuf, caw_ref, cab_ref, HALO_A, CONV_A_WIDTH, q, lb, r0)
            return run

        def _zx_task(c0):
            def run():
                res = _proj(wzx_ref, bzx_ref, c0)
                if c0 < D_INNER:
                    sz_ref[:, c0:c0 + COL_CHUNK] = _silu(res)
                else:
                    for j in range(COL_CHUNK // LANES):
                        xbuf[:, (c0 - D_INNER) // LANES + j, HALO_B:HALO_B + lt, :] = (
                            res[:, j * LANES:(j + 1) * LANES].reshape(nseq, lt, LANES))
            return run

        _run_interleaved(
            [_zx_task(c0) for c0 in range(0, D_INNER + D_XBC, COL_CHUNK)],
            [_conv_a_task(q, lb, r0) for q in range(nseq) for lb in range(D_A // LANES)
             for r0 in range(0, lt, conv_rb)])

        ln_rb = min(rows, CHUNK)

        def _norm_a_task(r0):
            def run():
                rs = slice(r0, r0 + ln_rb)
                va = _silu(_layer_norm(cv_ref[rs, :], nag_ref[...], nab_ref[...])) * ag_ref[rs, :]
                va_ref[rs, :] = va.astype(BF16)
            return run

        def _gate_task(c0):
            def run():
                sg_ref[:, c0:c0 + COL_CHUNK] = _sigmoid(_proj(wg_ref, bg_ref, c0))
            return run

        _run_interleaved([_gate_task(c0) for c0 in range(0, 3 * D_MODEL, COL_CHUNK)],
                         [_norm_a_task(r0) for r0 in range(0, rows, ln_rb)])

        def _conv_b_task(q, lb):
            def run():
                for r0 in range(0, lt, conv_rb):
                    xc_ref[q, r0:r0 + conv_rb, lb * LANES:(lb + 1) * LANES] = _silu(_dwconv_block(
                        xbuf, cbw_ref, cbb_ref, HALO_B, CONV_B_WIDTH, q, lb, r0))
            return run

        def _a_out_task(c0):
            def run():
                oa_ref[:, c0:c0 + COL_CHUNK] = _dot(va_ref[...], _w(waout_ref, 0, D_A, c0, c0 + COL_CHUNK))
            return run

        def _dt_task():
            dt4 = _softplus(_proj(wdt_ref, bdt_ref, 0, LANES) + dtb_ref[...])
            dt_ref[:, 0:lt, :] = dt4.reshape(nseq, lt, LANES)

        _run_interleaved([_a_out_task(c0) for c0 in range(0, D_MODEL, COL_CHUNK)] + [_dt_task],
                         [_conv_b_task(q, lb) for q in range(nseq) for lb in range(D_XBC // LANES)])
        if lp != lt:
            xc_ref[:, lt:lp, :] = jnp.zeros((nseq, lp - lt, D_XBC), F32)
            dt_ref[:, lt:lp, :] = jnp.zeros((nseq, lp - lt, LANES), F32)

    @pl.when(t == 0)
    def _init_state():
        if has_cache:
            ht_ref[...] = st_in[0].T
        else:
            ht_ref[...] = jnp.zeros(ht_ref.shape, F32)

    a_neg = -jnp.exp(alog_ref[...])
    lane_lo = lax.broadcasted_iota(jnp.int32, (1, LANES), 1) < CHUNK
    row_l = lax.broadcasted_iota(jnp.int32, (CHUNK, LANES), 0)
    col_s = lax.broadcasted_iota(jnp.int32, (CHUNK, LANES), 1) % CHUNK
    causal2 = row_l >= col_s
    bd_r = lax.broadcasted_iota(jnp.int32, (2 * CHUNK, LANES), 0) >= CHUNK
    bd_c = lax.broadcasted_iota(jnp.int32, (2 * CHUNK, LANES), 1) >= SSM_HEAD_DIM
    blockdiag = bd_r == bd_c
    lane4 = lax.broadcasted_iota(jnp.int32, (CHUNK, LANES), 1)
    tri3 = pltpu.bitcast(tri_ref[...], BF16)

    def _pieces(v4):
        v, r1, r2 = _split_residuals(v4)
        return jnp.where(lane4 < SSM_HEADS, v, jnp.where(lane4 < 2 * SSM_HEADS, r1,
                         jnp.where(lane4 < 3 * SSM_HEADS, r2, 0.0))).astype(BF16)

    k_acum, k_dt, k_dsd, k_expa = [], [], [], []
    for c in range(nch):
        dtc = dt_ref[s, c * CHUNK:(c + 1) * CHUNK, :]
        v, r1, r2 = _split_residuals(dtc * a_neg)
        stack = jnp.concatenate([v.astype(BF16), r1.astype(BF16), r2.astype(BF16)], axis=0)
        acum = _dot(tri3, stack)
        at_ref[c] = jnp.concatenate([acum, acum], axis=0).T
        k_acum.append(_pieces(acum))
        k_dt.append(_pieces(dtc))
        k_dsd.append(_pieces(dtc * jnp.exp(acum[CHUNK - 1:CHUNK, :] - acum)))
        k_expa.append(_pieces(jnp.exp(acum)))
    k_acum, k_dt, k_dsd, k_expa = [jnp.concatenate(k, axis=0) if nch > 1 else k[0]
                                   for k in (k_acum, k_dt, k_dsd, k_expa)]

    for g in range(SSM_GROUPS):
        gs = slice(g * GROUP_W, (g + 1) * GROUP_W)
        expand = _w(exp_ref, 0, LANES, g * GROUP_W, (g + 1) * GROUP_W)
        acum_e = _dot(k_acum, expand)
        expa_e = _dot(k_expa, expand)
        xs = xc_ref[s, :, gs]
        xdt = xs * _dot(k_dt, expand)
        xst = (xs * _dot(k_dsd, expand)).astype(BF16)
        h = ht_ref[:, gs]
        for c in range(nch):
            rs = slice(c * CHUNK, (c + 1) * CHUNK)
            bg = xc_ref[s, rs, D_INNER + g * SSM_STATE:D_INNER + (g + 1) * SSM_STATE].astype(BF16)
            cg = xc_ref[s, rs, D_INNER + (SSM_GROUPS + g) * SSM_STATE:
                        D_INNER + (SSM_GROUPS + g + 1) * SSM_STATE].astype(BF16)
            b2 = jnp.concatenate([bg, bg], axis=0)
            cb2 = lax.dot_general(cg, b2, (((1,), (1,)), ((), ())), preferred_element_type=F32)
            yoff = _dot(cg, h.astype(BF16))
            ydiag = []
            for j in range(PAIRS_PER_GROUP):
                pj = g * PAIRS_PER_GROUP + j
                cs = slice(j * LANES, (j + 1) * LANES)
                rowv = jnp.where(lane_lo, at_ref[c, 2 * pj:2 * pj + 1, :], at_ref[c, 2 * pj + 1:2 * pj + 2, :])
                lm = jnp.exp(jnp.where(causal2, acum_e[rs, cs] - rowv, -jnp.inf))
                mp = (cb2 * lm).astype(BF16)
                xp = xdt[rs, cs]
                xbd = jnp.where(blockdiag, jnp.concatenate([xp, xp], axis=0), 0.0).astype(BF16)
                ydiag.append(_dot(mp, xbd))
            decay = expa_e[rs, :]
            xc_ref[s, rs, gs] = (jnp.concatenate(ydiag, axis=1) + decay * yoff
                                 + dskip_ref[:, gs] * xs[rs, :])
            upd = lax.dot_general(bg, xst[rs, :], (((0,), (0,)), ((), ())), preferred_element_type=F32)
            h = h * decay[CHUNK - 1:CHUNK, :] + upd
        ht_ref[:, gs] = h

    @pl.when(t == nt - 1)
    def _final_state():
        st_ref[0] = ht_ref[...].T

    @pl.when(s == nseq - 1)
    def _dense_out():
        out_b = None
        for g in range(SSM_GROUPS):
            gs = slice(g * GROUP_W, (g + 1) * GROUP_W)
            gz = xc_ref[:, 0:lt, gs].reshape(rows, GROUP_W) * sz_ref[:, gs]
            gn = gz * lax.rsqrt(jnp.mean(gz * gz, axis=-1, keepdims=True) + LN_EPS) * gnw_ref[:, gs]
            part = _dot(gn.astype(BF16), _w(wbout_ref, g * GROUP_W, (g + 1) * GROUP_W, 0, D_MODEL))
            out_b = part if out_b is None else out_b + part

        merged = sg_ref[:, 0:D_MODEL] * oa_ref[...] + sg_ref[:, D_MODEL:2 * D_MODEL] * out_b
        pb = p_ref[...].reshape(rows, PLE_DIM).astype(BF16)
        ple = sg_ref[:, 2 * D_MODEL:3 * D_MODEL] * _dot(pb, _w(wple_ref, 0, PLE_DIM, 0, D_MODEL))
        r = (alpha * x_ref[...].reshape(rows, D_MODEL)
             + _dot(merged.astype(BF16), _w(wout_ref, 0, D_MODEL, 0, D_MODEL)) + ple)
        y_ref[...] = _layer_norm(r, lng_ref[...], lnb_ref[...]).reshape(nseq, lt, D_MODEL)

        @pl.when(t == nt - 1)
        def _finish():
            for lb in range(D_A // LANES):
                ca_ref[:, :, lb * LANES:(lb + 1) * LANES] = ubuf[:, lb, lt + HALO_A - hist_a:lt + HALO_A, :]
            for lb in range(D_XBC // LANES):
                cb_ref[:, :, lb * LANES:(lb + 1) * LANES] = xbuf[:, lb, lt + HALO_B - hist_b:lt + HALO_B, :]

        if nt > 1:
            ubuf[:, :, 0:HALO_A, :] = ubuf[:, :, lt:lt + HALO_A, :]
            xbuf[:, :, 0:HALO_B, :] = xbuf[:, :, lt:lt + HALO_B, :]


def _pack_rows(w):
    k, n = w.shape
    wb = w.astype(BF16).reshape(k // 2, 2, n)
    return lax.bitcast_convert_type(jnp.swapaxes(wb, 1, 2), jnp.uint32)


def _head_constants():
    k = np.arange(LANES)
    c = np.arange(D_INNER)
    expand = ((k[:, None] < 3 * SSM_HEADS) & ((k[:, None] % SSM_HEADS) == (c[None, :] // SSM_HEAD_DIM)))
    l = np.arange(CHUNK)
    kk = np.arange(3 * CHUNK)
    tri3 = (kk[None, :] % CHUNK) <= l[:, None]
    return _pack_rows(jnp.asarray(tri3, F32)), _pack_rows(jnp.asarray(expand, F32))


def _layer_call(x, p, caches, lw, *, layer, nseq, lt, alpha):
    nb, seq, _ = x.shape
    nt = seq // lt
    assert nseq == 1 or nt == 1
    lp = -(-lt // CHUNK) * CHUNK
    rows = nseq * lt
    has_cache = caches is not None
    tri3, expand = _head_constants()
    consts = list(lw) + [tri3, expand]

    def tile_spec(width):
        return pl.BlockSpec((nseq, lt, width), lambda b, t, s: (b, t, 0))

    def tile_seq_spec(r, width):
        return pl.BlockSpec((nseq, r, width), lambda b, t, s: (b, 0, 0))

    def state_spec():
        return pl.BlockSpec((1, D_INNER, SSM_STATE), lambda b, t, s: (b * nseq + s, 0, 0))

    def layer_spec(block, index_map):
        return pl.BlockSpec((None,) + block, lambda b, t, s: (layer,) + index_map(b, t, s))

    def whole(a):
        return pl.BlockSpec(a.shape, lambda b, t, s: (0,) * a.ndim)

    in_specs = [tile_spec(D_MODEL), layer_spec((nseq, lt, PLE_DIM), lambda b, t, s: (b, t, 0))]
    args = [x, p]
    if has_cache:
        in_specs += [layer_spec((nseq, CONV_A_WIDTH - 1, D_A), lambda b, t, s: (b, 0, 0)),
                     layer_spec((nseq, CONV_B_WIDTH - 1, D_XBC), lambda b, t, s: (b, 0, 0)),
                     layer_spec((1, D_INNER, SSM_STATE), lambda b, t, s: (b * nseq + s, 0, 0))]
        args += list(caches)
    in_specs += [whole(a) for a in consts]
    args += consts
    out_shape = (jax.ShapeDtypeStruct((nb, seq, D_MODEL), F32),
                 jax.ShapeDtypeStruct((nb, CONV_A_WIDTH - 1, D_A), F32),
                 jax.ShapeDtypeStruct((nb, CONV_B_WIDTH - 1, D_XBC), F32),
                 jax.ShapeDtypeStruct((nb, D_INNER, SSM_STATE), F32))
    out_specs = (tile_spec(D_MODEL), tile_seq_spec(CONV_A_WIDTH - 1, D_A),
                 tile_seq_spec(CONV_B_WIDTH - 1, D_XBC), state_spec())
    scratch = [pltpu.VMEM((nseq, D_A // LANES, HALO_A + lt, LANES), F32),
               pltpu.VMEM((nseq, D_XBC // LANES, HALO_B + lt, LANES), F32),
               pltpu.VMEM((rows, D_MODEL), BF16),
               pltpu.VMEM((rows, D_A), F32),
               pltpu.VMEM((rows, D_A), F32),
               pltpu.VMEM((rows, D_A), BF16),
               pltpu.VMEM((nseq, lp, D_XBC), F32),
               pltpu.VMEM((nseq, lp, LANES), F32),
               pltpu.VMEM((rows, D_INNER), F32),
               pltpu.VMEM((rows, 3 * D_MODEL), F32),
               pltpu.VMEM((rows, D_MODEL), F32),
               pltpu.VMEM((SSM_STATE, D_INNER), F32),
               pltpu.VMEM((lp // CHUNK, LANES, LANES), F32)]
    kern = functools.partial(_layer_kernel, nseq=nseq, lt=lt, nt=nt, has_cache=has_cache, alpha=alpha)
    return pl.pallas_call(
        kern,
        grid=(nb // nseq, nt, nseq),
        in_specs=in_specs,
        out_specs=out_specs,
        out_shape=out_shape,
        scratch_shapes=scratch,
        compiler_params=pltpu.CompilerParams(
            dimension_semantics=("arbitrary", "arbitrary", "arbitrary"),
            vmem_limit_bytes=VMEM_LIMIT_BYTES),
    )(*args)


def _prep_layer_weights(i, w_in, b_in, conv_a_w, conv_a_b, norm_a_g, norm_a_b, w_a_out, conv_b_w, conv_b_b,
                        dt_bias, a_log, d_skip, gnorm_w, w_b_out, w_out, w_ple, ln_g, ln_b):
    o_z = 3 * D_A
    o_dt = o_z + D_INNER + D_XBC
    o_g = o_dt + SSM_HEADS
    wi, bi = w_in[i], b_in[i]
    row = lambda v: v.reshape(1, -1)
    rep = lambda v: jnp.tile(v, HEAD_REP).reshape(1, LANES)
    return [
        _pack_rows(wi[:, 0:o_z]), row(bi[0:o_z]),
        _pack_rows(wi[:, o_z:o_dt]), row(bi[o_z:o_dt]),
        _pack_rows(jnp.tile(wi[:, o_dt:o_g], (1, HEAD_REP))), rep(bi[o_dt:o_g]),
        rep(dt_bias[i]), rep(a_log[i]),
        _pack_rows(wi[:, o_g:]), row(bi[o_g:]),
        conv_a_w[i], row(conv_a_b[i]), row(norm_a_g[i]), row(norm_a_b[i]), _pack_rows(w_a_out[i]),
        conv_b_w[i], row(conv_b_b[i]), row(jnp.repeat(d_skip[i], SSM_HEAD_DIM)), row(gnorm_w[i]),
        _pack_rows(w_b_out[i]), _pack_rows(w_out[i]), _pack_rows(w_ple[i]), row(ln_g[i]), row(ln_b[i]),
    ]


def kernel(x_prompt, x_sample, cache_conv_a, cache_conv_b, state_ssm, p_prompt, p_sample, w_in, b_in, conv_a_w, conv_a_b, norm_a_g, norm_a_b, w_a_out, conv_b_w, conv_b_b, dt_bias, a_log, d_skip, gnorm_w, w_b_out, w_out, w_ple, ln_g, ln_b):
    weights = (w_in, b_in, conv_a_w, conv_a_b, norm_a_g, norm_a_b, w_a_out, conv_b_w, conv_b_b,
               dt_bias, a_log, d_skip, gnorm_w, w_b_out, w_out, w_ple, ln_g, ln_b)
    depth = w_in.shape[0]
    alpha = float((2 * depth) ** 0.25)
    nb_s = x_sample.shape[0]
    caches = (cache_conv_a, cache_conv_b, state_ssm.reshape(depth, nb_s, D_INNER, SSM_STATE))
    xp, xs = x_prompt, x_sample
    outs_p, outs_s = [], []
    for i in range(depth):
        lw = _prep_layer_weights(i, *weights)
        xp, *rest_p = _layer_call(xp, p_prompt, None, lw, layer=i, nseq=1, lt=256, alpha=alpha)
        xs, *rest_s = _layer_call(xs, p_sample, caches, lw, layer=i, nseq=4, lt=x_sample.shape[1], alpha=alpha)
        outs_p.append(rest_p)
        outs_s.append(rest_s)

    def stacked(outs, nb):
        ca, cb, st = (jnp.stack(v) for v in zip(*outs))
        return ca, cb, st.reshape(depth, nb, SSM_HEADS, SSM_HEAD_DIM, SSM_STATE)

    ca_p, cb_p, h_p = stacked(outs_p, x_prompt.shape[0])
    ca_s, cb_s, h_s = stacked(outs_s, nb_s)
    return (xp, xs, ca_p, cb_p, h_p, ca_s, cb_s, h_s)
```

```python
import functools

import numpy as np
import jax
import jax.numpy as jnp
from jax import lax
from jax.experimental import pallas as pl
from jax.experimental.pallas import tpu as pltpu

F32 = jnp.float32
BF16 = jnp.bfloat16

D_MODEL = 1024
D_A = 1024
CONV_A_WIDTH = 31
D_INNER = 2048
SSM_HEAD_DIM = 64
SSM_HEADS = 32
SSM_GROUPS = 4
SSM_STATE = 128
CONV_B_WIDTH = 4
D_XBC = D_INNER + 2 * SSM_GROUPS * SSM_STATE
PLE_DIM = 256
CHUNK = 64
LN_EPS = 1e-5

LANES = 128
HEAD_REP = LANES // SSM_HEADS
GROUP_W = D_INNER // SSM_GROUPS
PAIRS_PER_GROUP = GROUP_W // LANES
HALO_A = 32
HALO_B = 8
COL_CHUNK = 1024
VMEM_LIMIT_BYTES = 60 * 1024 * 1024


def _sigmoid(v):
    return jax.nn.sigmoid(v)


def _silu(v):
    return v * jax.nn.sigmoid(v)


def _softplus(v):
    return jnp.maximum(v, 0.0) + jnp.log1p(jnp.exp(-jnp.abs(v)))


def _layer_norm(v, g, b):
    mu = jnp.mean(v, axis=-1, keepdims=True)
    c = v - mu
    var = jnp.mean(c * c, axis=-1, keepdims=True)
    return c * lax.rsqrt(var + LN_EPS) * g + b


def _dot(a, b):
    return jnp.dot(a, b, preferred_element_type=F32)


def _w(ref, r0, r1, c0, c1):
    return pltpu.bitcast(ref[r0 // 2:r1 // 2, c0:c1], BF16)


def _split_residuals(v):
    r1 = v - v.astype(BF16).astype(F32)
    r2 = r1 - r1.astype(BF16).astype(F32)
    return v, r1, r2


def _run_interleaved(a_tasks, b_tasks):
    na, nb = len(a_tasks), len(b_tasks)
    ia = ib = 0
    while ia < na or ib < nb:
        if ib >= nb or (ia < na and ia * nb <= ib * na):
            a_tasks[ia]()
            ia += 1
        else:
            b_tasks[ib]()
            ib += 1


def _layer_kernel(*refs, nseq, lt, nt, has_cache, alpha):
    it = iter(refs)
    x_ref, p_ref = next(it), next(it)
    if has_cache:
        ca_in, cb_in, st_in = next(it), next(it), next(it)
    (wa_ref, ba_ref, wzx_ref, bzx_ref, wdt_ref, bdt_ref, dtb_ref, alog_ref, wg_ref, bg_ref,
     caw_ref, cab_ref, nag_ref, nab_ref, waout_ref, cbw_ref, cbb_ref, dskip_ref, gnw_ref,
     wbout_ref, wout_ref, wple_ref, lng_ref, lnb_ref, tri_ref, exp_ref) = [next(it) for _ in range(26)]
    y_ref, ca_ref, cb_ref, st_ref = next(it), next(it), next(it), next(it)
    (ubuf, xbuf, xb_ref, cv_ref, ag_ref, va_ref, xc_ref, dt_ref, sz_ref, sg_ref, oa_ref,
     ht_ref, at_ref) = [next(it) for _ in range(13)]

    t = pl.program_id(1)
    s = pl.program_id(2)
    rows = nseq * lt
    lp = xc_ref.shape[1]
    nch = lp // CHUNK
    hist_a = CONV_A_WIDTH - 1
    hist_b = CONV_B_WIDTH - 1
    conv_rb = min(lt, CHUNK)

    def _dwconv_block(buf, w_ref, b_ref, halo, width, q, lb, r0):
        ls = slice(lb * LANES, (lb + 1) * LANES)
        acc = jnp.zeros((conv_rb, LANES), F32) + b_ref[:, ls]
        for k in range(width):
            off = halo - (width - 1) + k + r0
            acc = acc + buf[q, lb, off:off + conv_rb, :] * w_ref[k:k + 1, ls]
        return acc

    @pl.when(s == 0)
    def _dense_in():
        @pl.when(t == 0)
        def _init():
            ubuf[:, :, 0:HALO_A, :] = jnp.zeros((nseq, D_A // LANES, HALO_A, LANES), F32)
            xbuf[:, :, 0:HALO_B, :] = jnp.zeros((nseq, D_XBC // LANES, HALO_B, LANES), F32)
            if has_cache:
                for lb in range(D_A // LANES):
                    ubuf[:, lb, HALO_A - hist_a:HALO_A, :] = ca_in[:, :, lb * LANES:(lb + 1) * LANES]
                for lb in range(D_XBC // LANES):
                    xbuf[:, lb, HALO_B - hist_b:HALO_B, :] = cb_in[:, :, lb * LANES:(lb + 1) * LANES]

        xb_ref[...] = x_ref[...].reshape(rows, D_MODEL).astype(BF16)

        def _proj(w_ref, b_ref, c0, width=COL_CHUNK):
            return _dot(xb_ref[...], _w(w_ref, 0, D_MODEL, c0, c0 + width)) + b_ref[:, c0:c0 + width]

        for c0 in range(0, D_A, COL_CHUNK):
            u = _proj(wa_ref, ba_ref, c0) * _sigmoid(_proj(wa_ref, ba_ref, D_A + c0))
            for j in range(COL_CHUNK // LANES):
                ubuf[:, c0 // LANES + j, HALO_A:HALO_A + lt, :] = (
                    u[:, j * LANES:(j + 1) * LANES].reshape(nseq, lt, LANES))
            ag_ref[:, c0:c0 + COL_CHUNK] = _silu(_proj(wa_ref, ba_ref, 2 * D_A + c0))

        def _conv_a_task(q, lb, r0):
            def run():
                cv_ref[q * lt + r0:q * lt + r0 + conv_rb, lb * LANES:(lb + 1) * LANES] = _dwconv_block(
                    ubuf, caw_ref, cab_ref, HALO_A, CONV_A_WIDTH, q, lb, r0)
            return run

        def _zx_task(c0):
            def run():
                res = _proj(wzx_ref, bzx_ref, c0)
                if c0 < D_INNER:
                    sz_ref[:, c0:c0 + COL_CHUNK] = _silu(res)
                else:
                    for j in range(COL_CHUNK // LANES):
                        xbuf[:, (c0 - D_INNER) // LANES + j, HALO_B:HALO_B + lt, :] = (
                            res[:, j * LANES:(j + 1) * LANES].reshape(nseq, lt, LANES))
            return run

        _run_interleaved(
            [_zx_task(c0) for c0 in range(0, D_INNER + D_XBC, COL_CHUNK)],
            [_conv_a_task(q, lb, r0) for q in range(nseq) for lb in range(D_A // LANES)
             for r0 in range(0, lt, conv_rb)])

        ln_rb = min(rows, CHUNK)

        def _norm_a_task(r0):
            def run():
                rs = slice(r0, r0 + ln_rb)
                va = _silu(_layer_norm(cv_ref[rs, :], nag_ref[...], nab_ref[...])) * ag_ref[rs, :]
                va_ref[rs, :] = va.astype(BF16)
            return run

        def _gate_task(c0):
            def run():
                sg_ref[:, c0:c0 + COL_CHUNK] = _sigmoid(_proj(wg_ref, bg_ref, c0))
            return run

        _run_interleaved([_gate_task(c0) for c0 in range(0, 3 * D_MODEL, COL_CHUNK)],
                         [_norm_a_task(r0) for r0 in range(0, rows, ln_rb)])

        def _conv_b_task(q, lb):
            def run():
                for r0 in range(0, lt, conv_rb):
                    xc_ref[q, r0:r0 + conv_rb, lb * LANES:(lb + 1) * LANES] = _silu(_dwconv_block(
                        xbuf, cbw_ref, cbb_ref, HALO_B, CONV_B_WIDTH, q, lb, r0))
            return run

        def _a_out_task(c0):
            def run():
                oa_ref[:, c0:c0 + COL_CHUNK] = _dot(va_ref[...], _w(waout_ref, 0, D_A, c0, c0 + COL_CHUNK))
            return run

        def _dt_task():
            dt4 = _softplus(_proj(wdt_ref, bdt_ref, 0, LANES) + dtb_ref[...])
            dt_ref[:, 0:lt, :] = dt4.reshape(nseq, lt, LANES)

        _run_interleaved([_a_out_task(c0) for c0 in range(0, D_MODEL, COL_CHUNK)] + [_dt_task],
                         [_conv_b_task(q, lb) for q in range(nseq) for lb in range(D_XBC // LANES)])
        if lp != lt:
            xc_ref[:, lt:lp, :] = jnp.zeros((nseq, lp - lt, D_XBC), F32)
            dt_ref[:, lt:lp, :] = jnp.zeros((nseq, lp - lt, LANES), F32)

    @pl.when(t == 0)
    def _init_state():
        if has_cache:
            ht_ref[...] = st_in[0].T
        else:
            ht_ref[...] = jnp.zeros(ht_ref.shape, F32)

    a_neg = -jnp.exp(alog_ref[...])
    lane_lo = lax.broadcasted_iota(jnp.int32, (1, LANES), 1) < CHUNK
    row_l = lax.broadcasted_iota(jnp.int32, (CHUNK, LANES), 0)
    col_s = lax.broadcasted_iota(jnp.int32, (CHUNK, LANES), 1) % CHUNK
    causal2 = row_l >= col_s
    bd_r = lax.broadcasted_iota(jnp.int32, (2 * CHUNK, LANES), 0) >= CHUNK
    bd_c = lax.broadcasted_iota(jnp.int32, (2 * CHUNK, LANES), 1) >= SSM_HEAD_DIM
    blockdiag = bd_r == bd_c
    lane4 = lax.broadcasted_iota(jnp.int32, (CHUNK, LANES), 1)
    tri3 = pltpu.bitcast(tri_ref[...], BF16)

    def _pieces(v4):
        v, r1, r2 = _split_residuals(v4)
        return jnp.where(lane4 < SSM_HEADS, v, jnp.where(lane4 < 2 * SSM_HEADS, r1,
                         jnp.where(lane4 < 3 * SSM_HEADS, r2, 0.0))).astype(BF16)

    k_acum, k_dt, k_dsd = [], [], []
    for c in range(nch):
        dtc = dt_ref[s, c * CHUNK:(c + 1) * CHUNK, :]
        v, r1, r2 = _split_residuals(dtc * a_neg)
        stack = jnp.concatenate([v.astype(BF16), r1.astype(BF16), r2.astype(BF16)], axis=0)
        acum = _dot(tri3, stack)
        at_ref[c] = jnp.concatenate([acum, acum], axis=0).T
        k_acum.append(_pieces(acum))
        k_dt.append(_pieces(dtc))
        k_dsd.append(_pieces(dtc * jnp.exp(acum[CHUNK - 1:CHUNK, :] - acum)))
    k_acum, k_dt, k_dsd = [jnp.concatenate(k, axis=0) if nch > 1 else k[0] for k in (k_acum, k_dt, k_dsd)]

    for g in range(SSM_GROUPS):
        gs = slice(g * GROUP_W, (g + 1) * GROUP_W)
        expand = _w(exp_ref, 0, LANES, g * GROUP_W, (g + 1) * GROUP_W)
        acum_e = _dot(k_acum, expand)
        expa_e = jnp.exp(acum_e)
        xs = xc_ref[s, :, gs]
        xdt = xs * _dot(k_dt, expand)
        xst = (xs * _dot(k_dsd, expand)).astype(BF16)
        h = ht_ref[:, gs]
        for c in range(nch):
            rs = slice(c * CHUNK, (c + 1) * CHUNK)
            bg = xc_ref[s, rs, D_INNER + g * SSM_STATE:D_INNER + (g + 1) * SSM_STATE].astype(BF16)
            cg = xc_ref[s, rs, D_INNER + (SSM_GROUPS + g) * SSM_STATE:
                        D_INNER + (SSM_GROUPS + g + 1) * SSM_STATE].astype(BF16)
            b2 = jnp.concatenate([bg, bg], axis=0)
            cb2 = lax.dot_general(cg, b2, (((1,), (1,)), ((), ())), preferred_element_type=F32)
            yoff = _dot(cg, h.astype(BF16))
            ydiag = []
            for j in range(PAIRS_PER_GROUP):
                pj = g * PAIRS_PER_GROUP + j
                cs = slice(j * LANES, (j + 1) * LANES)
                rowv = jnp.where(lane_lo, at_ref[c, 2 * pj:2 * pj + 1, :], at_ref[c, 2 * pj + 1:2 * pj + 2, :])
                lm = jnp.exp(jnp.where(causal2, acum_e[rs, cs] - rowv, -jnp.inf))
                mp = (cb2 * lm).astype(BF16)
                xp = xdt[rs, cs]
                xbd = jnp.where(blockdiag, jnp.concatenate([xp, xp], axis=0), 0.0).astype(BF16)
                ydiag.append(_dot(mp, xbd))
            decay = expa_e[rs, :]
            xc_ref[s, rs, gs] = (jnp.concatenate(ydiag, axis=1) + decay * yoff
                                 + dskip_ref[:, gs] * xs[rs, :])
            upd = lax.dot_general(bg, xst[rs, :], (((0,), (0,)), ((), ())), preferred_element_type=F32)
            h = h * decay[CHUNK - 1:CHUNK, :] + upd
        ht_ref[:, gs] = h

    @pl.when(t == nt - 1)
    def _final_state():
        st_ref[0] = ht_ref[...].T

    @pl.when(s == nseq - 1)
    def _dense_out():
        out_b = None
        for g in range(SSM_GROUPS):
            gs = slice(g * GROUP_W, (g + 1) * GROUP_W)
            gz = xc_ref[:, 0:lt, gs].reshape(rows, GROUP_W) * sz_ref[:, gs]
            gn = gz * lax.rsqrt(jnp.mean(gz * gz, axis=-1, keepdims=True) + LN_EPS) * gnw_ref[:, gs]
            part = _dot(gn.astype(BF16), _w(wbout_ref, g * GROUP_W, (g + 1) * GROUP_W, 0, D_MODEL))
            out_b = part if out_b is None else out_b + part

        merged = sg_ref[:, 0:D_MODEL] * oa_ref[...] + sg_ref[:, D_MODEL:2 * D_MODEL] * out_b
        pb = p_ref[...].reshape(rows, PLE_DIM).astype(BF16)
        ple = sg_ref[:, 2 * D_MODEL:3 * D_MODEL] * _dot(pb, _w(wple_ref, 0, PLE_DIM, 0, D_MODEL))
        r = (alpha * x_ref[...].reshape(rows, D_MODEL)
             + _dot(merged.astype(BF16), _w(wout_ref, 0, D_MODEL, 0, D_MODEL)) + ple)
        y_ref[...] = _layer_norm(r, lng_ref[...], lnb_ref[...]).reshape(nseq, lt, D_MODEL)

        @pl.when(t == nt - 1)
        def _finish():
            for lb in range(D_A // LANES):
                ca_ref[:, :, lb * LANES:(lb + 1) * LANES] = ubuf[:, lb, lt + HALO_A - hist_a:lt + HALO_A, :]
            for lb in range(D_XBC // LANES):
                cb_ref[:, :, lb * LANES:(lb + 1) * LANES] = xbuf[:, lb, lt + HALO_B - hist_b:lt + HALO_B, :]

        if nt > 1:
            ubuf[:, :, 0:HALO_A, :] = ubuf[:, :, lt:lt + HALO_A, :]
            xbuf[:, :, 0:HALO_B, :] = xbuf[:, :, lt:lt + HALO_B, :]


def _pack_rows(w):
    bits = lax.bitcast_convert_type(w.astype(BF16), jnp.uint16).astype(jnp.uint32)
    return bits[0::2, :] | (bits[1::2, :] << 16)


def _head_constants():
    k = np.arange(LANES)
    c = np.arange(D_INNER)
    expand = ((k[:, None] < 3 * SSM_HEADS) & ((k[:, None] % SSM_HEADS) == (c[None, :] // SSM_HEAD_DIM)))
    l = np.arange(CHUNK)
    kk = np.arange(3 * CHUNK)
    tri3 = (kk[None, :] % CHUNK) <= l[:, None]
    return _pack_rows(jnp.asarray(tri3, F32)), _pack_rows(jnp.asarray(expand, F32))


def _layer_call(x, p, caches, lw, *, layer, nseq, lt, alpha):
    nb, seq, _ = x.shape
    nt = seq // lt
    assert nseq == 1 or nt == 1
    lp = -(-lt // CHUNK) * CHUNK
    rows = nseq * lt
    has_cache = caches is not None
    tri3, expand = _head_constants()
    consts = list(lw) + [tri3, expand]

    def tile_spec(width):
        return pl.BlockSpec((nseq, lt, width), lambda b, t, s: (b, t, 0))

    def tile_seq_spec(r, width):
        return pl.BlockSpec((nseq, r, width), lambda b, t, s: (b, 0, 0))

    def state_spec():
        return pl.BlockSpec((1, D_INNER, SSM_STATE), lambda b, t, s: (b * nseq + s, 0, 0))

    def layer_spec(block, index_map):
        return pl.BlockSpec((None,) + block, lambda b, t, s: (layer,) + index_map(b, t, s))

    def whole(a):
        return pl.BlockSpec(a.shape, lambda b, t, s: (0,) * a.ndim)

    in_specs = [tile_spec(D_MODEL), layer_spec((nseq, lt, PLE_DIM), lambda b, t, s: (b, t, 0))]
    args = [x, p]
    if has_cache:
        in_specs += [layer_spec((nseq, CONV_A_WIDTH - 1, D_A), lambda b, t, s: (b, 0, 0)),
                     layer_spec((nseq, CONV_B_WIDTH - 1, D_XBC), lambda b, t, s: (b, 0, 0)),
                     layer_spec((1, D_INNER, SSM_STATE), lambda b, t, s: (b * nseq + s, 0, 0))]
        args += list(caches)
    in_specs += [whole(a) for a in consts]
    args += consts
    out_shape = (jax.ShapeDtypeStruct((nb, seq, D_MODEL), F32),
                 jax.ShapeDtypeStruct((nb, CONV_A_WIDTH - 1, D_A), F32),
                 jax.ShapeDtypeStruct((nb, CONV_B_WIDTH - 1, D_XBC), F32),
                 jax.ShapeDtypeStruct((nb, D_INNER, SSM_STATE), F32))
    out_specs = (tile_spec(D_MODEL), tile_seq_spec(CONV_A_WIDTH - 1, D_A),
                 tile_seq_spec(CONV_B_WIDTH - 1, D_XBC), state_spec())
    scratch = [pltpu.VMEM((nseq, D_A // LANES, HALO_A + lt, LANES), F32),
               pltpu.VMEM((nseq, D_XBC // LANES, HALO_B + lt, LANES), F32),
               pltpu.VMEM((rows, D_MODEL), BF16),
               pltpu.VMEM((rows, D_A), F32),
               pltpu.VMEM((rows, D_A), F32),
               pltpu.VMEM((rows, D_A), BF16),
               pltpu.VMEM((nseq, lp, D_XBC), F32),
               pltpu.VMEM((nseq, lp, LANES), F32),
               pltpu.VMEM((rows, D_INNER), F32),
               pltpu.VMEM((rows, 3 * D_MODEL), F32),
               pltpu.VMEM((rows, D_MODEL), F32),
               pltpu.VMEM((SSM_STATE, D_INNER), F32),
               pltpu.VMEM((lp // CHUNK, LANES, LANES), F32)]
    kern = functools.partial(_layer_kernel, nseq=nseq, lt=lt, nt=nt, has_cache=has_cache, alpha=alpha)
    return pl.pallas_call(
        kern,
        grid=(nb // nseq, nt, nseq),
        in_specs=in_specs,
        out_specs=out_specs,
        out_shape=out_shape,
        scratch_shapes=scratch,
        compiler_params=pltpu.CompilerParams(
            dimension_semantics=("arbitrary", "arbitrary", "arbitrary"),
            vmem_limit_bytes=VMEM_LIMIT_BYTES),
    )(*args)


def _prep_layer_weights(i, w_in, b_in, conv_a_w, conv_a_b, norm_a_g, norm_a_b, w_a_out, conv_b_w, conv_b_b,
                        dt_bias, a_log, d_skip, gnorm_w, w_b_out, w_out, w_ple, ln_g, ln_b):
    o_z = 3 * D_A
    o_dt = o_z + D_INNER + D_XBC
    o_g = o_dt + SSM_HEADS
    wi, bi = w_in[i], b_in[i]
    row = lambda v: v.reshape(1, -1)
    rep = lambda v: jnp.tile(v, HEAD_REP).reshape(1, LANES)
    return [
        _pack_rows(wi[:, 0:o_z]), row(bi[0:o_z]),
        _pack_rows(wi[:, o_z:o_dt]), row(bi[o_z:o_dt]),
        _pack_rows(jnp.tile(wi[:, o_dt:o_g], (1, HEAD_REP))), rep(bi[o_dt:o_g]),
        rep(dt_bias[i]), rep(a_log[i]),
        _pack_rows(wi[:, o_g:]), row(bi[o_g:]),
        conv_a_w[i], row(conv_a_b[i]), row(norm_a_g[i]), row(norm_a_b[i]), _pack_rows(w_a_out[i]),
        conv_b_w[i], row(conv_b_b[i]), row(jnp.repeat(d_skip[i], SSM_HEAD_DIM)), row(gnorm_w[i]),
        _pack_rows(w_b_out[i]), _pack_rows(w_out[i]), _pack_rows(w_ple[i]), row(ln_g[i]), row(ln_b[i]),
    ]


def kernel(x_prompt, x_sample, cache_conv_a, cache_conv_b, state_ssm, p_prompt, p_sample, w_in, b_in, conv_a_w, conv_a_b, norm_a_g, norm_a_b, w_a_out, conv_b_w, conv_b_b, dt_bias, a_log, d_skip, gnorm_w, w_b_out, w_out, w_ple, ln_g, ln_b):
    weights = (w_in, b_in, conv_a_w, conv_a_b, norm_a_g, norm_a_b, w_a_out, conv_b_w, conv_b_b,
               dt_bias, a_log, d_skip, gnorm_w, w_b_out, w_out, w_ple, ln_g, ln_b)
    depth = w_in.shape[0]
    alpha = float((2 * depth) ** 0.25)
    nb_s = x_sample.shape[0]
    caches = (cache_conv_a, cache_conv_b, state_ssm.reshape(depth, nb_s, D_INNER, SSM_STATE))
    xp, xs = x_prompt, x_sample
    outs_p, outs_s = [], []
    for i in range(depth):
        lw = _prep_layer_weights(i, *weights)
        xp, *rest_p = _layer_call(xp, p_prompt, None, lw, layer=i, nseq=1, lt=256, alpha=alpha)
        xs, *rest_s = _layer_call(xs, p_sample, caches, lw, layer=i, nseq=4, lt=x_sample.shape[1], alpha=alpha)
        outs_p.append(rest_p)
        outs_s.append(rest_s)

    def stacked(outs, nb):
        ca, cb, st = (jnp.stack(v) for v in zip(*outs))
        return ca, cb, st.reshape(depth, nb, SSM_HEADS, SSM_HEAD_DIM, SSM_STATE)

    ca_p, cb_p, h_p = stacked(outs_p, x_prompt.shape[0])
    ca_s, cb_s, h_s = stacked(outs_s, nb_s)
    return (xp, xs, ca_p, cb_p, h_p, ca_s, cb_s, h_s)
```

```python
import functools

import numpy as np
import jax
import jax.numpy as jnp
from jax import lax
from jax.experimental import pallas as pl
from jax.experimental.pallas import tpu as pltpu

F32 = jnp.float32
BF16 = jnp.bfloat16

D_MODEL = 1024
D_A = 1024
CONV_A_WIDTH = 31
D_INNER = 2048
SSM_HEAD_DIM = 64
SSM_HEADS = 32
SSM_GROUPS = 4
SSM_STATE = 128
CONV_B_WIDTH = 4
D_XBC = D_INNER + 2 * SSM_GROUPS * SSM_STATE
PLE_DIM = 256
CHUNK = 64
LN_EPS = 1e-5

LANES = 128
HEAD_REP = LANES // SSM_HEADS
GROUP_W = D_INNER // SSM_GROUPS
PAIRS_PER_GROUP = GROUP_W // LANES
HALO_A = 32
HALO_B = 8
COL_CHUNK = 1024
PACK_COLS = 512
VMEM_LIMIT_BYTES = 60 * 1024 * 1024


def _sigmoid(v):
    return jax.nn.sigmoid(v)


def _silu(v):
    return v * jax.nn.sigmoid(v)


def _softplus(v):
    return jnp.maximum(v, 0.0) + jnp.log1p(jnp.exp(-jnp.abs(v)))


def _layer_norm(v, g, b):
    mu = jnp.mean(v, axis=-1, keepdims=True)
    c = v - mu
    var = jnp.mean(c * c, axis=-1, keepdims=True)
    return c * lax.rsqrt(var + LN_EPS) * g + b


def _dot(a, b):
    return jnp.dot(a, b, preferred_element_type=F32)


def _w(ref, r0, r1, c0, c1):
    return pltpu.bitcast(ref[r0 // 2:r1 // 2, c0:c1], BF16)


def _split_residuals(v):
    r1 = v - v.astype(BF16).astype(F32)
    r2 = r1 - r1.astype(BF16).astype(F32)
    return v, r1, r2


def _run_interleaved(a_tasks, b_tasks):
    na, nb = len(a_tasks), len(b_tasks)
    ia = ib = 0
    while ia < na or ib < nb:
        if ib >= nb or (ia < na and ia * nb <= ib * na):
            a_tasks[ia]()
            ia += 1
        else:
            b_tasks[ib]()
            ib += 1


def _layer_kernel(*refs, nseq, lt, nt, has_cache, alpha):
    it = iter(refs)
    x_ref, p_ref = next(it), next(it)
    if has_cache:
        ca_in, cb_in, st_in = next(it), next(it), next(it)
    (wa_ref, ba_ref, wzx_ref, bzx_ref, wdt_ref, bdt_ref, dtb_ref, alog_ref, wg_ref, bg_ref,
     caw_ref, cab_ref, nag_ref, nab_ref, waout_ref, cbw_ref, cbb_ref, dskip_ref, gnw_ref,
     wbout_ref, wout_ref, wple_ref, lng_ref, lnb_ref, tri_ref, exp_ref) = [next(it) for _ in range(26)]
    y_ref, ca_ref, cb_ref, st_ref = next(it), next(it), next(it), next(it)
    (ubuf, xbuf, xb_ref, cv_ref, ag_ref, va_ref, xc_ref, dt_ref, sz_ref, sg_ref, oa_ref,
     ht_ref, at_ref) = [next(it) for _ in range(13)]

    t = pl.program_id(1)
    s = pl.program_id(2)
    rows = nseq * lt
    lp = xc_ref.shape[1]
    nch = lp // CHUNK
    hist_a = CONV_A_WIDTH - 1
    hist_b = CONV_B_WIDTH - 1
    conv_rb = min(lt, CHUNK)

    def _dwconv_block(buf, w_ref, b_ref, halo, width, q, lb, r0):
        ls = slice(lb * LANES, (lb + 1) * LANES)
        acc = jnp.zeros((conv_rb, LANES), F32) + b_ref[:, ls]
        for k in range(width):
            off = halo - (width - 1) + k + r0
            acc = acc + buf[q, lb, off:off + conv_rb, :] * w_ref[k:k + 1, ls]
        return acc

    @pl.when(s == 0)
    def _dense_in():
        @pl.when(t == 0)
        def _init():
            ubuf[:, :, 0:HALO_A, :] = jnp.zeros((nseq, D_A // LANES, HALO_A, LANES), F32)
            xbuf[:, :, 0:HALO_B, :] = jnp.zeros((nseq, D_XBC // LANES, HALO_B, LANES), F32)
            if has_cache:
                for lb in range(D_A // LANES):
                    ubuf[:, lb, HALO_A - hist_a:HALO_A, :] = ca_in[:, :, lb * LANES:(lb + 1) * LANES]
                for lb in range(D_XBC // LANES):
                    xbuf[:, lb, HALO_B - hist_b:HALO_B, :] = cb_in[:, :, lb * LANES:(lb + 1) * LANES]

        xb_ref[...] = x_ref[...].reshape(rows, D_MODEL).astype(BF16)

        def _proj(w_ref, b_ref, c0, width=COL_CHUNK):
            return _dot(xb_ref[...], _w(w_ref, 0, D_MODEL, c0, c0 + width)) + b_ref[:, c0:c0 + width]

        for c0 in range(0, D_A, COL_CHUNK):
            u = _proj(wa_ref, ba_ref, c0) * _sigmoid(_proj(wa_ref, ba_ref, D_A + c0))
            for j in range(COL_CHUNK // LANES):
                ubuf[:, c0 // LANES + j, HALO_A:HALO_A + lt, :] = (
                    u[:, j * LANES:(j + 1) * LANES].reshape(nseq, lt, LANES))
            ag_ref[:, c0:c0 + COL_CHUNK] = _silu(_proj(wa_ref, ba_ref, 2 * D_A + c0))

        def _conv_a_task(q, lb, r0):
            def run():
                cv_ref[q * lt + r0:q * lt + r0 + conv_rb, lb * LANES:(lb + 1) * LANES] = _dwconv_block(
                    ubuf, caw_ref, cab_ref, HALO_A, CONV_A_WIDTH, q, lb, r0)
            return run

        def _zx_task(c0):
            def run():
                res = _proj(wzx_ref, bzx_ref, c0)
                if c0 < D_INNER:
                    sz_ref[:, c0:c0 + COL_CHUNK] = _silu(res)
                else:
                    for j in range(COL_CHUNK // LANES):
                        xbuf[:, (c0 - D_INNER) // LANES + j, HALO_B:HALO_B + lt, :] = (
                            res[:, j * LANES:(j + 1) * LANES].reshape(nseq, lt, LANES))
            return run

        _run_interleaved(
            [_zx_task(c0) for c0 in range(0, D_INNER + D_XBC, COL_CHUNK)],
            [_conv_a_task(q, lb, r0) for q in range(nseq) for lb in range(D_A // LANES)
             for r0 in range(0, lt, conv_rb)])

        ln_rb = min(rows, CHUNK)

        def _norm_a_task(r0):
            def run():
                rs = slice(r0, r0 + ln_rb)
                va = _silu(_layer_norm(cv_ref[rs, :], nag_ref[...], nab_ref[...])) * ag_ref[rs, :]
                va_ref[rs, :] = va.astype(BF16)
            return run

        def _gate_task(c0):
            def run():
                sg_ref[:, c0:c0 + COL_CHUNK] = _sigmoid(_proj(wg_ref, bg_ref, c0))
            return run

        _run_interleaved([_gate_task(c0) for c0 in range(0, 3 * D_MODEL, COL_CHUNK)],
                         [_norm_a_task(r0) for r0 in range(0, rows, ln_rb)])

        def _conv_b_task(q, lb):
            def run():
                for r0 in range(0, lt, conv_rb):
                    xc_ref[q, r0:r0 + conv_rb, lb * LANES:(lb + 1) * LANES] = _silu(_dwconv_block(
                        xbuf, cbw_ref, cbb_ref, HALO_B, CONV_B_WIDTH, q, lb, r0))
            return run

        def _a_out_task(c0):
            def run():
                oa_ref[:, c0:c0 + COL_CHUNK] = _dot(va_ref[...], _w(waout_ref, 0, D_A, c0, c0 + COL_CHUNK))
            return run

        def _dt_task():
            dt4 = _softplus(_proj(wdt_ref, bdt_ref, 0, LANES) + dtb_ref[...])
            dt_ref[:, 0:lt, :] = dt4.reshape(nseq, lt, LANES)

        _run_interleaved([_a_out_task(c0) for c0 in range(0, D_MODEL, COL_CHUNK)] + [_dt_task],
                         [_conv_b_task(q, lb) for q in range(nseq) for lb in range(D_XBC // LANES)])
        if lp != lt:
            xc_ref[:, lt:lp, :] = jnp.zeros((nseq, lp - lt, D_XBC), F32)
            dt_ref[:, lt:lp, :] = jnp.zeros((nseq, lp - lt, LANES), F32)

    @pl.when(t == 0)
    def _init_state():
        if has_cache:
            ht_ref[...] = st_in[0].T
        else:
            ht_ref[...] = jnp.zeros(ht_ref.shape, F32)

    a_neg = -jnp.exp(alog_ref[...])
    lane_lo = lax.broadcasted_iota(jnp.int32, (1, LANES), 1) < CHUNK
    row_l = lax.broadcasted_iota(jnp.int32, (CHUNK, LANES), 0)
    col_s = lax.broadcasted_iota(jnp.int32, (CHUNK, LANES), 1) % CHUNK
    causal2 = row_l >= col_s
    bd_r = lax.broadcasted_iota(jnp.int32, (2 * CHUNK, LANES), 0) >= CHUNK
    bd_c = lax.broadcasted_iota(jnp.int32, (2 * CHUNK, LANES), 1) >= SSM_HEAD_DIM
    blockdiag = bd_r == bd_c
    lane4 = lax.broadcasted_iota(jnp.int32, (CHUNK, LANES), 1)
    tri3 = pltpu.bitcast(tri_ref[...], BF16)

    def _pieces(v4):
        v, r1, r2 = _split_residuals(v4)
        return jnp.where(lane4 < SSM_HEADS, v, jnp.where(lane4 < 2 * SSM_HEADS, r1,
                         jnp.where(lane4 < 3 * SSM_HEADS, r2, 0.0))).astype(BF16)

    k_acum, k_dt, k_dsd = [], [], []
    for c in range(nch):
        dtc = dt_ref[s, c * CHUNK:(c + 1) * CHUNK, :]
        v, r1, r2 = _split_residuals(dtc * a_neg)
        stack = jnp.concatenate([v.astype(BF16), r1.astype(BF16), r2.astype(BF16)], axis=0)
        acum = _dot(tri3, stack)
        at_ref[c] = jnp.concatenate([acum, acum], axis=0).T
        k_acum.append(_pieces(acum))
        k_dt.append(_pieces(dtc))
        k_dsd.append(_pieces(dtc * jnp.exp(acum[CHUNK - 1:CHUNK, :] - acum)))
    k_acum, k_dt, k_dsd = [jnp.concatenate(k, axis=0) if nch > 1 else k[0] for k in (k_acum, k_dt, k_dsd)]

    for g in range(SSM_GROUPS):
        gs = slice(g * GROUP_W, (g + 1) * GROUP_W)
        expand = _w(exp_ref, 0, LANES, g * GROUP_W, (g + 1) * GROUP_W)
        acum_e = _dot(k_acum, expand)
        expa_e = jnp.exp(acum_e)
        xs = xc_ref[s, :, gs]
        xdt = xs * _dot(k_dt, expand)
        xst = (xs * _dot(k_dsd, expand)).astype(BF16)
        h = ht_ref[:, gs]
        for c in range(nch):
            rs = slice(c * CHUNK, (c + 1) * CHUNK)
            bg = xc_ref[s, rs, D_INNER + g * SSM_STATE:D_INNER + (g + 1) * SSM_STATE].astype(BF16)
            cg = xc_ref[s, rs, D_INNER + (SSM_GROUPS + g) * SSM_STATE:
                        D_INNER + (SSM_GROUPS + g + 1) * SSM_STATE].astype(BF16)
            b2 = jnp.concatenate([bg, bg], axis=0)
            cb2 = lax.dot_general(cg, b2, (((1,), (1,)), ((), ())), preferred_element_type=F32)
            yoff = _dot(cg, h.astype(BF16))
            ydiag = []
            for j in range(PAIRS_PER_GROUP):
                pj = g * PAIRS_PER_GROUP + j
                cs = slice(j * LANES, (j + 1) * LANES)
                rowv = jnp.where(lane_lo, at_ref[c, 2 * pj:2 * pj + 1, :], at_ref[c, 2 * pj + 1:2 * pj + 2, :])
                lm = jnp.exp(jnp.where(causal2, acum_e[rs, cs] - rowv, -jnp.inf))
                mp = (cb2 * lm).astype(BF16)
                xp = xdt[rs, cs]
                xbd = jnp.where(blockdiag, jnp.concatenate([xp, xp], axis=0), 0.0).astype(BF16)
                ydiag.append(_dot(mp, xbd))
            decay = expa_e[rs, :]
            xc_ref[s, rs, gs] = (jnp.concatenate(ydiag, axis=1) + decay * yoff
                                 + dskip_ref[:, gs] * xs[rs, :])
            upd = lax.dot_general(bg, xst[rs, :], (((0,), (0,)), ((), ())), preferred_element_type=F32)
            h = h * decay[CHUNK - 1:CHUNK, :] + upd
        ht_ref[:, gs] = h

    @pl.when(t == nt - 1)
    def _final_state():
        st_ref[0] = ht_ref[...].T

    @pl.when(s == nseq - 1)
    def _dense_out():
        out_b = None
        for g in range(SSM_GROUPS):
            gs = slice(g * GROUP_W, (g + 1) * GROUP_W)
            gz = xc_ref[:, 0:lt, gs].reshape(rows, GROUP_W) * sz_ref[:, gs]
            gn = gz * lax.rsqrt(jnp.mean(gz * gz, axis=-1, keepdims=True) + LN_EPS) * gnw_ref[:, gs]
            part = _dot(gn.astype(BF16), _w(wbout_ref, g * GROUP_W, (g + 1) * GROUP_W, 0, D_MODEL))
            out_b = part if out_b is None else out_b + part

        merged = sg_ref[:, 0:D_MODEL] * oa_ref[...] + sg_ref[:, D_MODEL:2 * D_MODEL] * out_b
        pb = p_ref[...].reshape(rows, PLE_DIM).astype(BF16)
        ple = sg_ref[:, 2 * D_MODEL:3 * D_MODEL] * _dot(pb, _w(wple_ref, 0, PLE_DIM, 0, D_MODEL))
        r = (alpha * x_ref[...].reshape(rows, D_MODEL)
             + _dot(merged.astype(BF16), _w(wout_ref, 0, D_MODEL, 0, D_MODEL)) + ple)
        y_ref[...] = _layer_norm(r, lng_ref[...], lnb_ref[...]).reshape(nseq, lt, D_MODEL)

        @pl.when(t == nt - 1)
        def _finish():
            for lb in range(D_A // LANES):
                ca_ref[:, :, lb * LANES:(lb + 1) * LANES] = ubuf[:, lb, lt + HALO_A - hist_a:lt + HALO_A, :]
            for lb in range(D_XBC // LANES):
                cb_ref[:, :, lb * LANES:(lb + 1) * LANES] = xbuf[:, lb, lt + HALO_B - hist_b:lt + HALO_B, :]

        if nt > 1:
            ubuf[:, :, 0:HALO_A, :] = ubuf[:, :, lt:lt + HALO_A, :]
            xbuf[:, :, 0:HALO_B, :] = xbuf[:, :, lt:lt + HALO_B, :]


def _pack_kernel(w_ref, o_ref):
    o_ref[...] = pltpu.bitcast(w_ref[...].astype(BF16), jnp.uint32)


def _pack_weight(w, layer, col0, ncols):
    _, k, _ = w.shape
    bw = min(PACK_COLS, ncols)
    assert col0 % bw == 0 and ncols % bw == 0
    return pl.pallas_call(
        _pack_kernel,
        grid=(ncols // bw,),
        in_specs=[pl.BlockSpec((None, k, bw), lambda j: (layer, 0, col0 // bw + j))],
        out_specs=pl.BlockSpec((k // 2, bw), lambda j: (0, j)),
        out_shape=jax.ShapeDtypeStruct((k // 2, ncols), jnp.uint32),
        compiler_params=pltpu.CompilerParams(dimension_semantics=("arbitrary",)),
    )(w)


def _pack_rows_const(m):
    bits = np.where(m, 0x3F80, 0).astype(np.uint32)
    return jnp.asarray(bits[0::2, :] | (bits[1::2, :] << 16), jnp.uint32)


def _head_constants():
    k = np.arange(LANES)
    c = np.arange(D_INNER)
    expand = ((k[:, None] < 3 * SSM_HEADS) & ((k[:, None] % SSM_HEADS) == (c[None, :] // SSM_HEAD_DIM)))
    l = np.arange(CHUNK)
    kk = np.arange(3 * CHUNK)
    tri3 = (kk[None, :] % CHUNK) <= l[:, None]
    return _pack_rows_const(tri3), _pack_rows_const(expand)


def _layer_call(x, p, caches, lw, *, layer, nseq, lt, alpha):
    nb, seq, _ = x.shape
    nt = seq // lt
    assert nseq == 1 or nt == 1
    lp = -(-lt // CHUNK) * CHUNK
    rows = nseq * lt
    has_cache = caches is not None
    tri3, expand = _head_constants()
    consts = list(lw) + [tri3, expand]

    def tile_spec(width):
        return pl.BlockSpec((nseq, lt, width), lambda b, t, s: (b, t, 0))

    def tile_seq_spec(r, width):
        return pl.BlockSpec((nseq, r, width), lambda b, t, s: (b, 0, 0))

    def state_spec():
        return pl.BlockSpec((1, D_INNER, SSM_STATE), lambda b, t, s: (b * nseq + s, 0, 0))

    def layer_spec(block, index_map):
        return pl.BlockSpec((None,) + block, lambda b, t, s: (layer,) + index_map(b, t, s))

    def whole(a):
        return pl.BlockSpec(a.shape, lambda b, t, s: (0,) * a.ndim)

    in_specs = [tile_spec(D_MODEL), layer_spec((nseq, lt, PLE_DIM), lambda b, t, s: (b, t, 0))]
    args = [x, p]
    if has_cache:
        in_specs += [layer_spec((nseq, CONV_A_WIDTH - 1, D_A), lambda b, t, s: (b, 0, 0)),
                     layer_spec((nseq, CONV_B_WIDTH - 1, D_XBC), lambda b, t, s: (b, 0, 0)),
                     layer_spec((1, D_INNER, SSM_STATE), lambda b, t, s: (b * nseq + s, 0, 0))]
        args += list(caches)
    in_specs += [whole(a) for a in consts]
    args += consts
    out_shape = (jax.ShapeDtypeStruct((nb, seq, D_MODEL), F32),
                 jax.ShapeDtypeStruct((nb, CONV_A_WIDTH - 1, D_A), F32),
                 jax.ShapeDtypeStruct((nb, CONV_B_WIDTH - 1, D_XBC), F32),
                 jax.ShapeDtypeStruct((nb, D_INNER, SSM_STATE), F32))
    out_specs = (tile_spec(D_MODEL), tile_seq_spec(CONV_A_WIDTH - 1, D_A),
                 tile_seq_spec(CONV_B_WIDTH - 1, D_XBC), state_spec())
    scratch = [pltpu.VMEM((nseq, D_A // LANES, HALO_A + lt, LANES), F32),
               pltpu.VMEM((nseq, D_XBC // LANES, HALO_B + lt, LANES), F32),
               pltpu.VMEM((rows, D_MODEL), BF16),
               pltpu.VMEM((rows, D_A), F32),
               pltpu.VMEM((rows, D_A), F32),
               pltpu.VMEM((rows, D_A), BF16),
               pltpu.VMEM((nseq, lp, D_XBC), F32),
               pltpu.VMEM((nseq, lp, LANES), F32),
               pltpu.VMEM((rows, D_INNER), F32),
               pltpu.VMEM((rows, 3 * D_MODEL), F32),
               pltpu.VMEM((rows, D_MODEL), F32),
               pltpu.VMEM((SSM_STATE, D_INNER), F32),
               pltpu.VMEM((lp // CHUNK, LANES, LANES), F32)]
    kern = functools.partial(_layer_kernel, nseq=nseq, lt=lt, nt=nt, has_cache=has_cache, alpha=alpha)
    return pl.pallas_call(
        kern,
        grid=(nb // nseq, nt, nseq),
        in_specs=in_specs,
        out_specs=out_specs,
        out_shape=out_shape,
        scratch_shapes=scratch,
        compiler_params=pltpu.CompilerParams(
            dimension_semantics=("arbitrary", "arbitrary", "arbitrary"),
            vmem_limit_bytes=VMEM_LIMIT_BYTES),
    )(*args)


O_Z = 3 * D_A
O_DT = O_Z + D_INNER + D_XBC
O_G = O_DT + SSM_HEADS


def _prep_layer_weights(i, w_dt_rep, w_gates, w_in, b_in, conv_a_w, conv_a_b, norm_a_g, norm_a_b, w_a_out,
                        conv_b_w, conv_b_b, dt_bias, a_log, d_skip, gnorm_w, w_b_out, w_out, w_ple, ln_g, ln_b):
    bi = b_in[i]
    row = lambda v: v.reshape(1, -1)
    rep = lambda v: jnp.tile(v, HEAD_REP).reshape(1, LANES)
    return [
        _pack_weight(w_in, i, 0, O_Z), row(bi[0:O_Z]),
        _pack_weight(w_in, i, O_Z, O_DT - O_Z), row(bi[O_Z:O_DT]),
        _pack_weight(w_dt_rep, i, 0, LANES), rep(bi[O_DT:O_G]),
        rep(dt_bias[i]), rep(a_log[i]),
        _pack_weight(w_gates, i, 0, 3 * D_MODEL), row(bi[O_G:]),
        conv_a_w[i], row(conv_a_b[i]), row(norm_a_g[i]), row(norm_a_b[i]), _pack_weight(w_a_out, i, 0, D_MODEL),
        conv_b_w[i], row(conv_b_b[i]), row(jnp.repeat(d_skip[i], SSM_HEAD_DIM)), row(gnorm_w[i]),
        _pack_weight(w_b_out, i, 0, D_MODEL), _pack_weight(w_out, i, 0, D_MODEL),
        _pack_weight(w_ple, i, 0, D_MODEL), row(ln_g[i]), row(ln_b[i]),
    ]


def kernel(x_prompt, x_sample, cache_conv_a, cache_conv_b, state_ssm, p_prompt, p_sample, w_in, b_in, conv_a_w, conv_a_b, norm_a_g, norm_a_b, w_a_out, conv_b_w, conv_b_b, dt_bias, a_log, d_skip, gnorm_w, w_b_out, w_out, w_ple, ln_g, ln_b):
    weights = (w_in, b_in, conv_a_w, conv_a_b, norm_a_g, norm_a_b, w_a_out, conv_b_w, conv_b_b,
               dt_bias, a_log, d_skip, gnorm_w, w_b_out, w_out, w_ple, ln_g, ln_b)
    depth = w_in.shape[0]
    alpha = float((2 * depth) ** 0.25)
    nb_s = x_sample.shape[0]
    caches = (cache_conv_a, cache_conv_b, state_ssm.reshape(depth, nb_s, D_INNER, SSM_STATE))
    w_dt_rep = jnp.tile(w_in[:, :, O_DT:O_G], (1, 1, HEAD_REP))
    w_gates = w_in[:, :, O_G:]
    xp, xs = x_prompt, x_sample
    outs_p, outs_s = [], []
    for i in range(depth):
        lw = _prep_layer_weights(i, w_dt_rep, w_gates, *weights)
        xp, *rest_p = _layer_call(xp, p_prompt, None, lw, layer=i, nseq=1, lt=256, alpha=alpha)
        xs, *rest_s = _layer_call(xs, p_sample, caches, lw, layer=i, nseq=4, lt=x_sample.shape[1], alpha=alpha)
        outs_p.append(rest_p)
        outs_s.append(rest_s)

    def stacked(outs, nb):
        ca, cb, st = (jnp.stack(v) for v in zip(*outs))
        return ca, cb, st.reshape(depth, nb, SSM_HEADS, SSM_HEAD_DIM, SSM_STATE)

    ca_p, cb_p, h_p = stacked(outs_p, x_prompt.shape[0])
    ca_s, cb_s, h_s = stacked(outs_s, nb_s)
    return (xp, xs, ca_p, cb_p, h_p, ca_s, cb_s, h_s)
```

```python
import functools

import numpy as np
import jax
import jax.numpy as jnp
from jax import lax
from jax.experimental import pallas as pl
from jax.experimental.pallas import tpu as pltpu

F32 = jnp.float32
BF16 = jnp.bfloat16

D_MODEL = 1024
D_A = 1024
CONV_A_WIDTH = 31
D_INNER = 2048
SSM_HEAD_DIM = 64
SSM_HEADS = 32
SSM_GROUPS = 4
SSM_STATE = 128
CONV_B_WIDTH = 4
D_XBC = D_INNER + 2 * SSM_GROUPS * SSM_STATE
PLE_DIM = 256
CHUNK = 64
LN_EPS = 1e-5

LANES = 128
HEAD_REP = LANES // SSM_HEADS
GROUP_W = D_INNER // SSM_GROUPS
PAIRS_PER_GROUP = GROUP_W // LANES
HALO_A = 32
HALO_B = 8
COL_CHUNK = 1024
PACK_COLS = 512
VMEM_LIMIT_BYTES = 60 * 1024 * 1024


def _sigmoid(v):
    return jax.nn.sigmoid(v)


def _silu(v):
    return v * jax.nn.sigmoid(v)


def _softplus(v):
    return jnp.maximum(v, 0.0) + jnp.log1p(jnp.exp(-jnp.abs(v)))


def _layer_norm(v, g, b):
    mu = jnp.mean(v, axis=-1, keepdims=True)
    c = v - mu
    var = jnp.mean(c * c, axis=-1, keepdims=True)
    return c * lax.rsqrt(var + LN_EPS) * g + b


def _dot(a, b):
    return jnp.dot(a, b, preferred_element_type=F32)


def _w(ref, r0, r1, c0, c1):
    return pltpu.bitcast(ref[r0 // 2:r1 // 2, c0:c1], BF16)


def _split_residuals(v):
    r1 = v - v.astype(BF16).astype(F32)
    r2 = r1 - r1.astype(BF16).astype(F32)
    return v, r1, r2


def _run_interleaved(a_tasks, b_tasks):
    na, nb = len(a_tasks), len(b_tasks)
    ia = ib = 0
    while ia < na or ib < nb:
        if ib >= nb or (ia < na and ia * nb <= ib * na):
            a_tasks[ia]()
            ia += 1
        else:
            b_tasks[ib]()
            ib += 1


def _layer_kernel(*refs, nseq, lt, nt, has_cache, alpha):
    it = iter(refs)
    x_ref, p_ref = next(it), next(it)
    if has_cache:
        ca_in, cb_in, st_in = next(it), next(it), next(it)
    (wa_ref, ba_ref, wzx_ref, bzx_ref, wdt_ref, bdt_ref, dtb_ref, alog_ref, wg_ref, bg_ref,
     caw_ref, cab_ref, nag_ref, nab_ref, waout_ref, cbw_ref, cbb_ref, dskip_ref, gnw_ref,
     wbout_ref, wout_ref, wple_ref, lng_ref, lnb_ref, tri_ref, exp_ref) = [next(it) for _ in range(26)]
    y_ref, ca_ref, cb_ref, st_ref = next(it), next(it), next(it), next(it)
    (ubuf, xbuf, xb_ref, cv_ref, ag_ref, va_ref, xc_ref, dt_ref, sz_ref, sg_ref, oa_ref,
     ht_ref, at_ref) = [next(it) for _ in range(13)]

    t = pl.program_id(1)
    s = pl.program_id(2)
    rows = nseq * lt
    lp = xc_ref.shape[1]
    nch = lp // CHUNK
    hist_a = CONV_A_WIDTH - 1
    hist_b = CONV_B_WIDTH - 1
    conv_rb = min(lt, CHUNK)

    def _dwconv_block(buf, w_ref, b_ref, halo, width, q, lb, r0):
        ls = slice(lb * LANES, (lb + 1) * LANES)
        acc = jnp.zeros((conv_rb, LANES), F32) + b_ref[:, ls]
        for k in range(width):
            off = halo - (width - 1) + k + r0
            acc = acc + buf[q, lb, off:off + conv_rb, :] * w_ref[k:k + 1, ls]
        return acc

    @pl.when(s == 0)
    def _dense_in():
        @pl.when(t == 0)
        def _init():
            ubuf[:, :, 0:HALO_A, :] = jnp.zeros((nseq, D_A // LANES, HALO_A, LANES), F32)
            xbuf[:, :, 0:HALO_B, :] = jnp.zeros((nseq, D_XBC // LANES, HALO_B, LANES), F32)
            if has_cache:
                for lb in range(D_A // LANES):
                    ubuf[:, lb, HALO_A - hist_a:HALO_A, :] = ca_in[:, :, lb * LANES:(lb + 1) * LANES]
                for lb in range(D_XBC // LANES):
                    xbuf[:, lb, HALO_B - hist_b:HALO_B, :] = cb_in[:, :, lb * LANES:(lb + 1) * LANES]

        xb_ref[...] = x_ref[...].reshape(rows, D_MODEL).astype(BF16)

        def _proj(w_ref, b_ref, c0, width=COL_CHUNK):
            return _dot(xb_ref[...], _w(w_ref, 0, D_MODEL, c0, c0 + width)) + b_ref[:, c0:c0 + width]

        for c0 in range(0, D_A, COL_CHUNK):
            u = _proj(wa_ref, ba_ref, c0) * _sigmoid(_proj(wa_ref, ba_ref, D_A + c0))
            for j in range(COL_CHUNK // LANES):
                ubuf[:, c0 // LANES + j, HALO_A:HALO_A + lt, :] = (
                    u[:, j * LANES:(j + 1) * LANES].reshape(nseq, lt, LANES))
            ag_ref[:, c0:c0 + COL_CHUNK] = _silu(_proj(wa_ref, ba_ref, 2 * D_A + c0))

        def _conv_a_task(q, lb, r0):
            def run():
                cv_ref[q * lt + r0:q * lt + r0 + conv_rb, lb * LANES:(lb + 1) * LANES] = _dwconv_block(
                    ubuf, caw_ref, cab_ref, HALO_A, CONV_A_WIDTH, q, lb, r0)
            return run

        def _zx_task(c0):
            def run():
                res = _proj(wzx_ref, bzx_ref, c0)
                if c0 < D_INNER:
                    sz_ref[:, c0:c0 + COL_CHUNK] = _silu(res)
                else:
                    for j in range(COL_CHUNK // LANES):
                        xbuf[:, (c0 - D_INNER) // LANES + j, HALO_B:HALO_B + lt, :] = (
                            res[:, j * LANES:(j + 1) * LANES].reshape(nseq, lt, LANES))
            return run

        _run_interleaved(
            [_zx_task(c0) for c0 in range(0, D_INNER + D_XBC, COL_CHUNK)],
            [_conv_a_task(q, lb, r0) for q in range(nseq) for lb in range(D_A // LANES)
             for r0 in range(0, lt, conv_rb)])

        ln_rb = min(rows, CHUNK)

        def _norm_a_task(r0):
            def run():
                rs = slice(r0, r0 + ln_rb)
                va = _silu(_layer_norm(cv_ref[rs, :], nag_ref[...], nab_ref[...])) * ag_ref[rs, :]
                va_ref[rs, :] = va.astype(BF16)
            return run

        def _gate_task(c0):
            def run():
                sg_ref[:, c0:c0 + COL_CHUNK] = _sigmoid(_proj(wg_ref, bg_ref, c0))
            return run

        _run_interleaved([_gate_task(c0) for c0 in range(0, 3 * D_MODEL, COL_CHUNK)],
                         [_norm_a_task(r0) for r0 in range(0, rows, ln_rb)])

        def _conv_b_task(q, lb):
            def run():
                for r0 in range(0, lt, conv_rb):
                    xc_ref[q, r0:r0 + conv_rb, lb * LANES:(lb + 1) * LANES] = _silu(_dwconv_block(
                        xbuf, cbw_ref, cbb_ref, HALO_B, CONV_B_WIDTH, q, lb, r0))
            return run

        def _a_out_task(c0):
            def run():
                oa_ref[:, c0:c0 + COL_CHUNK] = _dot(va_ref[...], _w(waout_ref, 0, D_A, c0, c0 + COL_CHUNK))
            return run

        def _dt_task():
            dt4 = _softplus(_proj(wdt_ref, bdt_ref, 0, LANES) + dtb_ref[...])
            dt_ref[:, 0:lt, :] = dt4.reshape(nseq, lt, LANES)

        _run_interleaved([_a_out_task(c0) for c0 in range(0, D_MODEL, COL_CHUNK)] + [_dt_task],
                         [_conv_b_task(q, lb) for q in range(nseq) for lb in range(D_XBC // LANES)])
        if lp != lt:
            xc_ref[:, lt:lp, :] = jnp.zeros((nseq, lp - lt, D_XBC), F32)
            dt_ref[:, lt:lp, :] = jnp.zeros((nseq, lp - lt, LANES), F32)

    @pl.when(t == 0)
    def _init_state():
        if has_cache:
            ht_ref[...] = st_in[0].T
        else:
            ht_ref[...] = jnp.zeros(ht_ref.shape, F32)

    a_neg = -jnp.exp(alog_ref[...])
    lane_lo = lax.broadcasted_iota(jnp.int32, (1, LANES), 1) < CHUNK
    row_l = lax.broadcasted_iota(jnp.int32, (CHUNK, LANES), 0)
    col_s = lax.broadcasted_iota(jnp.int32, (CHUNK, LANES), 1) % CHUNK
    causal2 = row_l >= col_s
    bd_r = lax.broadcasted_iota(jnp.int32, (2 * CHUNK, LANES), 0) >= CHUNK
    bd_c = lax.broadcasted_iota(jnp.int32, (2 * CHUNK, LANES), 1) >= SSM_HEAD_DIM
    blockdiag = bd_r == bd_c
    lane4 = lax.broadcasted_iota(jnp.int32, (CHUNK, LANES), 1)
    tri3 = pltpu.bitcast(tri_ref[...], BF16)

    def _pieces(v4):
        v, r1, r2 = _split_residuals(v4)
        return jnp.where(lane4 < SSM_HEADS, v, jnp.where(lane4 < 2 * SSM_HEADS, r1,
                         jnp.where(lane4 < 3 * SSM_HEADS, r2, 0.0))).astype(BF16)

    k_acum, k_dt, k_dsd = [], [], []
    for c in range(nch):
        dtc = dt_ref[s, c * CHUNK:(c + 1) * CHUNK, :]
        v, r1, r2 = _split_residuals(dtc * a_neg)
        stack = jnp.concatenate([v.astype(BF16), r1.astype(BF16), r2.astype(BF16)], axis=0)
        acum = _dot(tri3, stack)
        at_ref[c] = jnp.concatenate([acum, acum], axis=0).T
        k_acum.append(_pieces(acum))
        k_dt.append(_pieces(dtc))
        k_dsd.append(_pieces(dtc * jnp.exp(acum[CHUNK - 1:CHUNK, :] - acum)))
    k_acum, k_dt, k_dsd = [jnp.concatenate(k, axis=0) if nch > 1 else k[0] for k in (k_acum, k_dt, k_dsd)]

    for g in range(SSM_GROUPS):
        gs = slice(g * GROUP_W, (g + 1) * GROUP_W)
        expand = _w(exp_ref, 0, LANES, g * GROUP_W, (g + 1) * GROUP_W)
        acum_e = _dot(k_acum, expand)
        expa_e = jnp.exp(acum_e)
        xs = xc_ref[s, :, gs]
        xdt = xs * _dot(k_dt, expand)
        xst = (xs * _dot(k_dsd, expand)).astype(BF16)
        h = ht_ref[:, gs]
        for c in range(nch):
            rs = slice(c * CHUNK, (c + 1) * CHUNK)
            bg = xc_ref[s, rs, D_INNER + g * SSM_STATE:D_INNER + (g + 1) * SSM_STATE].astype(BF16)
            cg = xc_ref[s, rs, D_INNER + (SSM_GROUPS + g) * SSM_STATE:
                        D_INNER + (SSM_GROUPS + g + 1) * SSM_STATE].astype(BF16)
            b2 = jnp.concatenate([bg, bg], axis=0)
            cb2 = lax.dot_general(cg, b2, (((1,), (1,)), ((), ())), preferred_element_type=F32)
            yoff = _dot(cg, h.astype(BF16))
            ydiag = []
            for j in range(PAIRS_PER_GROUP):
                pj = g * PAIRS_PER_GROUP + j
                cs = slice(j * LANES, (j + 1) * LANES)
                rowv = jnp.where(lane_lo, at_ref[c, 2 * pj:2 * pj + 1, :], at_ref[c, 2 * pj + 1:2 * pj + 2, :])
                lm = jnp.exp(jnp.where(causal2, acum_e[rs, cs] - rowv, -jnp.inf))
                mp = (cb2 * lm).astype(BF16)
                xp = xdt[rs, cs]
                xbd = jnp.where(blockdiag, jnp.concatenate([xp, xp], axis=0), 0.0).astype(BF16)
                ydiag.append(_dot(mp, xbd))
            decay = expa_e[rs, :]
            xc_ref[s, rs, gs] = (jnp.concatenate(ydiag, axis=1) + decay * yoff
                                 + dskip_ref[:, gs] * xs[rs, :])
            upd = lax.dot_general(bg, xst[rs, :], (((0,), (0,)), ((), ())), preferred_element_type=F32)
            h = h * decay[CHUNK - 1:CHUNK, :] + upd
        ht_ref[:, gs] = h

    @pl.when(t == nt - 1)
    def _final_state():
        st_ref[0] = ht_ref[...].T

    @pl.when(s == nseq - 1)
    def _dense_out():
        out_b = None
        for g in range(SSM_GROUPS):
            gs = slice(g * GROUP_W, (g + 1) * GROUP_W)
            gz = xc_ref[:, 0:lt, gs].reshape(rows, GROUP_W) * sz_ref[:, gs]
            gn = gz * lax.rsqrt(jnp.mean(gz * gz, axis=-1, keepdims=True) + LN_EPS) * gnw_ref[:, gs]
            part = _dot(gn.astype(BF16), _w(wbout_ref, g * GROUP_W, (g + 1) * GROUP_W, 0, D_MODEL))
            out_b = part if out_b is None else out_b + part

        merged = sg_ref[:, 0:D_MODEL] * oa_ref[...] + sg_ref[:, D_MODEL:2 * D_MODEL] * out_b
        pb = p_ref[...].reshape(rows, PLE_DIM).astype(BF16)
        ple = sg_ref[:, 2 * D_MODEL:3 * D_MODEL] * _dot(pb, _w(wple_ref, 0, PLE_DIM, 0, D_MODEL))
        r = (alpha * x_ref[...].reshape(rows, D_MODEL)
             + _dot(merged.astype(BF16), _w(wout_ref, 0, D_MODEL, 0, D_MODEL)) + ple)
        y_ref[...] = _layer_norm(r, lng_ref[...], lnb_ref[...]).reshape(nseq, lt, D_MODEL)

        @pl.when(t == nt - 1)
        def _finish():
            for lb in range(D_A // LANES):
                ca_ref[:, :, lb * LANES:(lb + 1) * LANES] = ubuf[:, lb, lt + HALO_A - hist_a:lt + HALO_A, :]
            for lb in range(D_XBC // LANES):
                cb_ref[:, :, lb * LANES:(lb + 1) * LANES] = xbuf[:, lb, lt + HALO_B - hist_b:lt + HALO_B, :]

        if nt > 1:
            ubuf[:, :, 0:HALO_A, :] = ubuf[:, :, lt:lt + HALO_A, :]
            xbuf[:, :, 0:HALO_B, :] = xbuf[:, :, lt:lt + HALO_B, :]


def _pack_kernel(w_ref, o_ref):
    o_ref[...] = pltpu.bitcast(w_ref[...], jnp.uint32)


def _pack_weight(w, layer, col0, ncols):
    _, k, _ = w.shape
    bw = min(PACK_COLS, ncols)
    assert col0 % bw == 0 and ncols % bw == 0
    return pl.pallas_call(
        _pack_kernel,
        grid=(ncols // bw,),
        in_specs=[pl.BlockSpec((None, k, bw), lambda j: (layer, 0, col0 // bw + j))],
        out_specs=pl.BlockSpec((k // 2, bw), lambda j: (0, j)),
        out_shape=jax.ShapeDtypeStruct((k // 2, ncols), jnp.uint32),
        compiler_params=pltpu.CompilerParams(dimension_semantics=("arbitrary",)),
    )(w)


def _pack_rows_const(m):
    bits = np.where(m, 0x3F80, 0).astype(np.uint32)
    return jnp.asarray(bits[0::2, :] | (bits[1::2, :] << 16), jnp.uint32)


def _head_constants():
    k = np.arange(LANES)
    c = np.arange(D_INNER)
    expand = ((k[:, None] < 3 * SSM_HEADS) & ((k[:, None] % SSM_HEADS) == (c[None, :] // SSM_HEAD_DIM)))
    l = np.arange(CHUNK)
    kk = np.arange(3 * CHUNK)
    tri3 = (kk[None, :] % CHUNK) <= l[:, None]
    return _pack_rows_const(tri3), _pack_rows_const(expand)


def _layer_call(x, p, caches, lw, *, layer, nseq, lt, alpha):
    nb, seq, _ = x.shape
    nt = seq // lt
    assert nseq == 1 or nt == 1
    lp = -(-lt // CHUNK) * CHUNK
    rows = nseq * lt
    has_cache = caches is not None
    tri3, expand = _head_constants()
    consts = list(lw) + [tri3, expand]

    def tile_spec(width):
        return pl.BlockSpec((nseq, lt, width), lambda b, t, s: (b, t, 0))

    def tile_seq_spec(r, width):
        return pl.BlockSpec((nseq, r, width), lambda b, t, s: (b, 0, 0))

    def state_spec():
        return pl.BlockSpec((1, D_INNER, SSM_STATE), lambda b, t, s: (b * nseq + s, 0, 0))

    def layer_spec(block, index_map):
        return pl.BlockSpec((None,) + block, lambda b, t, s: (layer,) + index_map(b, t, s))

    def whole(a):
        return pl.BlockSpec(a.shape, lambda b, t, s: (0,) * a.ndim)

    in_specs = [tile_spec(D_MODEL), layer_spec((nseq, lt, PLE_DIM), lambda b, t, s: (b, t, 0))]
    args = [x, p]
    if has_cache:
        in_specs += [layer_spec((nseq, CONV_A_WIDTH - 1, D_A), lambda b, t, s: (b, 0, 0)),
                     layer_spec((nseq, CONV_B_WIDTH - 1, D_XBC), lambda b, t, s: (b, 0, 0)),
                     layer_spec((1, D_INNER, SSM_STATE), lambda b, t, s: (b * nseq + s, 0, 0))]
        args += list(caches)
    in_specs += [whole(a) for a in consts]
    args += consts
    out_shape = (jax.ShapeDtypeStruct((nb, seq, D_MODEL), F32),
                 jax.ShapeDtypeStruct((nb, CONV_A_WIDTH - 1, D_A), F32),
                 jax.ShapeDtypeStruct((nb, CONV_B_WIDTH - 1, D_XBC), F32),
                 jax.ShapeDtypeStruct((nb, D_INNER, SSM_STATE), F32))
    out_specs = (tile_spec(D_MODEL), tile_seq_spec(CONV_A_WIDTH - 1, D_A),
                 tile_seq_spec(CONV_B_WIDTH - 1, D_XBC), state_spec())
    scratch = [pltpu.VMEM((nseq, D_A // LANES, HALO_A + lt, LANES), F32),
               pltpu.VMEM((nseq, D_XBC // LANES, HALO_B + lt, LANES), F32),
               pltpu.VMEM((rows, D_MODEL), BF16),
               pltpu.VMEM((rows, D_A), F32),
               pltpu.VMEM((rows, D_A), F32),
               pltpu.VMEM((rows, D_A), BF16),
               pltpu.VMEM((nseq, lp, D_XBC), F32),
               pltpu.VMEM((nseq, lp, LANES), F32),
               pltpu.VMEM((rows, D_INNER), F32),
               pltpu.VMEM((rows, 3 * D_MODEL), F32),
               pltpu.VMEM((rows, D_MODEL), F32),
               pltpu.VMEM((SSM_STATE, D_INNER), F32),
               pltpu.VMEM((lp // CHUNK, LANES, LANES), F32)]
    kern = functools.partial(_layer_kernel, nseq=nseq, lt=lt, nt=nt, has_cache=has_cache, alpha=alpha)
    return pl.pallas_call(
        kern,
        grid=(nb // nseq, nt, nseq),
        in_specs=in_specs,
        out_specs=out_specs,
        out_shape=out_shape,
        scratch_shapes=scratch,
        compiler_params=pltpu.CompilerParams(
            dimension_semantics=("arbitrary", "arbitrary", "arbitrary"),
            vmem_limit_bytes=VMEM_LIMIT_BYTES),
    )(*args)


O_Z = 3 * D_A
O_DT = O_Z + D_INNER + D_XBC
O_G = O_DT + SSM_HEADS


def _cast_matrices(w_in, w_a_out, w_b_out, w_out, w_ple):
    return (w_in[:, :, 0:O_DT].astype(BF16),
            jnp.tile(w_in[:, :, O_DT:O_G], (1, 1, HEAD_REP)).astype(BF16),
            w_in[:, :, O_G:].astype(BF16),
            w_a_out.astype(BF16), w_b_out.astype(BF16), w_out.astype(BF16), w_ple.astype(BF16))


def _prep_layer_weights(i, mats, b_in, conv_a_w, conv_a_b, norm_a_g, norm_a_b, conv_b_w, conv_b_b,
                        dt_bias, a_log, d_skip, gnorm_w, ln_g, ln_b):
    wb_azx, wb_dt, wb_g, wb_a_out, wb_b_out, wb_out, wb_ple = mats
    bi = b_in[i]
    row = lambda v: v.reshape(1, -1)
    rep = lambda v: jnp.tile(v, HEAD_REP).reshape(1, LANES)
    return [
        _pack_weight(wb_azx, i, 0, O_Z), row(bi[0:O_Z]),
        _pack_weight(wb_azx, i, O_Z, O_DT - O_Z), row(bi[O_Z:O_DT]),
        _pack_weight(wb_dt, i, 0, LANES), rep(bi[O_DT:O_G]),
        rep(dt_bias[i]), rep(a_log[i]),
        _pack_weight(wb_g, i, 0, 3 * D_MODEL), row(bi[O_G:]),
        conv_a_w[i], row(conv_a_b[i]), row(norm_a_g[i]), row(norm_a_b[i]), _pack_weight(wb_a_out, i, 0, D_MODEL),
        conv_b_w[i], row(conv_b_b[i]), row(jnp.repeat(d_skip[i], SSM_HEAD_DIM)), row(gnorm_w[i]),
        _pack_weight(wb_b_out, i, 0, D_MODEL), _pack_weight(wb_out, i, 0, D_MODEL),
        _pack_weight(wb_ple, i, 0, D_MODEL), row(ln_g[i]), row(ln_b[i]),
    ]


def kernel(x_prompt, x_sample, cache_conv_a, cache_conv_b, state_ssm, p_prompt, p_sample, w_in, b_in, conv_a_w, conv_a_b, norm_a_g, norm_a_b, w_a_out, conv_b_w, conv_b_b, dt_bias, a_log, d_skip, gnorm_w, w_b_out, w_out, w_ple, ln_g, ln_b):
    vectors = (b_in, conv_a_w, conv_a_b, norm_a_g, norm_a_b, conv_b_w, conv_b_b,
               dt_bias, a_log, d_skip, gnorm_w, ln_g, ln_b)
    mats = _cast_matrices(w_in, w_a_out, w_b_out, w_out, w_ple)
    depth = w_in.shape[0]
    alpha = float((2 * depth) ** 0.25)
    nb_s = x_sample.shape[0]
    caches = (cache_conv_a, cache_conv_b, state_ssm.reshape(depth, nb_s, D_INNER, SSM_STATE))
    xp, xs = x_prompt, x_sample
    outs_p, outs_s = [], []
    for i in range(depth):
        lw = _prep_layer_weights(i, mats, *vectors)
        xp, *rest_p = _layer_call(xp, p_prompt, None, lw, layer=i, nseq=1, lt=256, alpha=alpha)
        xs, *rest_s = _layer_call(xs, p_sample, caches, lw, layer=i, nseq=4, lt=x_sample.shape[1], alpha=alpha)
        outs_p.append(rest_p)
        outs_s.append(rest_s)

    def stacked(outs, nb):
        ca, cb, st = (jnp.stack(v) for v in zip(*outs))
        return ca, cb, st.reshape(depth, nb, SSM_HEADS, SSM_HEAD_DIM, SSM_STATE)

    ca_p, cb_p, h_p = stacked(outs_p, x_prompt.shape[0])
    ca_s, cb_s, h_s = stacked(outs_s, nb_s)
    return (xp, xs, ca_p, cb_p, h_p, ca_s, cb_s, h_s)
```

```python
import functools

import numpy as np
import jax
import jax.numpy as jnp
from jax import lax
from jax.experimental import pallas as pl
from jax.experimental.pallas import tpu as pltpu

F32 = jnp.float32
BF16 = jnp.bfloat16

D_MODEL = 1024
D_A = 1024
CONV_A_WIDTH = 31
D_INNER = 2048
SSM_HEAD_DIM = 64
SSM_HEADS = 32
SSM_GROUPS = 4
SSM_STATE = 128
CONV_B_WIDTH = 4
D_XBC = D_INNER + 2 * SSM_GROUPS * SSM_STATE
PLE_DIM = 256
CHUNK = 64
LN_EPS = 1e-5

LANES = 128
HEAD_REP = LANES // SSM_HEADS
GROUP_W = D_INNER // SSM_GROUPS
PAIRS_PER_GROUP = GROUP_W // LANES
HALO_A = 32
TAIL_A = 8
HALO_B = 8
COL_CHUNK = 1024
PACK_COLS = 512
VMEM_LIMIT_BYTES = 60 * 1024 * 1024


def _sigmoid(v):
    return jax.nn.sigmoid(v)


def _silu(v):
    return v * jax.nn.sigmoid(v)


def _softplus(v):
    return jnp.maximum(v, 0.0) + jnp.log1p(jnp.exp(-jnp.abs(v)))


def _layer_norm(v, g, b):
    mu = jnp.mean(v, axis=-1, keepdims=True)
    c = v - mu
    var = jnp.mean(c * c, axis=-1, keepdims=True)
    return c * lax.rsqrt(var + LN_EPS) * g + b


def _dot(a, b):
    return jnp.dot(a, b, preferred_element_type=F32)


def _w(ref, r0, r1, c0, c1):
    return pltpu.bitcast(ref[r0 // 2:r1 // 2, c0:c1], BF16)


def _split_residuals(v):
    r1 = v - v.astype(BF16).astype(F32)
    r2 = r1 - r1.astype(BF16).astype(F32)
    return v, r1, r2


def _run_interleaved(a_tasks, b_tasks):
    na, nb = len(a_tasks), len(b_tasks)
    ia = ib = 0
    while ia < na or ib < nb:
        if ib >= nb or (ia < na and ia * nb <= ib * na):
            a_tasks[ia]()
            ia += 1
        else:
            b_tasks[ib]()
            ib += 1


def _layer_kernel(*refs, nseq, lt, nt, has_cache, alpha):
    it = iter(refs)
    x_ref, p_ref = next(it), next(it)
    if has_cache:
        ca_in, cb_in, st_in = next(it), next(it), next(it)
    (wa_ref, ba_ref, wzx_ref, bzx_ref, wdt_ref, bdt_ref, dtb_ref, alog_ref, wg_ref, bg_ref,
     caw_ref, cab_ref, nag_ref, nab_ref, waout_ref, cbw_ref, cbb_ref, dskip_ref, gnw_ref,
     wbout_ref, wout_ref, wple_ref, lng_ref, lnb_ref, tri_ref, exp_ref) = [next(it) for _ in range(26)]
    y_ref, ca_ref, cb_ref, st_ref = next(it), next(it), next(it), next(it)
    (ubuf, ue_ref, uo_ref, xbuf, xb_ref, cv_ref, ag_ref, va_ref, xc_ref, dt_ref, sz_ref, sg_ref, oa_ref,
     ht_ref, at_ref) = [next(it) for _ in range(15)]

    t = pl.program_id(1)
    s = pl.program_id(2)
    rows = nseq * lt
    lp = xc_ref.shape[1]
    nch = lp // CHUNK
    hist_a = CONV_A_WIDTH - 1
    hist_b = CONV_B_WIDTH - 1
    conv_rb = min(lt, CHUNK)

    def _dwconv_block(buf, w_ref, b_ref, halo, width, q, lb, r0):
        ls = slice(lb * LANES, (lb + 1) * LANES)
        acc = jnp.zeros((conv_rb, LANES), F32) + b_ref[:, ls]
        for k in range(width):
            off = halo - (width - 1) + k + r0
            acc = acc + buf[q, lb, off:off + conv_rb, :] * w_ref[k:k + 1, ls]
        return acc

    def _conv_a_block(q, lb, r0):
        ls = slice(lb * LANES, (lb + 1) * LANES)
        acc = jnp.zeros((conv_rb, LANES), BF16)
        for k in range(CONV_A_WIDTH):
            off = HALO_A - hist_a + k + r0
            src = ue_ref if off % 2 == 0 else uo_ref
            win = pltpu.bitcast(src[q, lb, off // 2:off // 2 + conv_rb // 2, :], BF16)
            acc = acc + win * caw_ref[k:k + 1, ls].astype(BF16)
        return acc.astype(F32) + cab_ref[:, ls]

    @pl.when(s == 0)
    def _dense_in():
        @pl.when(t == 0)
        def _init():
            ubuf[:, :, 0:HALO_A, :] = jnp.zeros((nseq, D_A // LANES, HALO_A, LANES), F32)
            ubuf[:, :, HALO_A + lt:HALO_A + lt + TAIL_A, :] = jnp.zeros((nseq, D_A // LANES, TAIL_A, LANES), F32)
            xbuf[:, :, 0:HALO_B, :] = jnp.zeros((nseq, D_XBC // LANES, HALO_B, LANES), F32)
            if has_cache:
                for lb in range(D_A // LANES):
                    ubuf[:, lb, HALO_A - hist_a:HALO_A, :] = ca_in[:, :, lb * LANES:(lb + 1) * LANES]
                for lb in range(D_XBC // LANES):
                    xbuf[:, lb, HALO_B - hist_b:HALO_B, :] = cb_in[:, :, lb * LANES:(lb + 1) * LANES]

        xb_ref[...] = x_ref[...].reshape(rows, D_MODEL).astype(BF16)

        def _proj(w_ref, b_ref, c0, width=COL_CHUNK):
            return _dot(xb_ref[...], _w(w_ref, 0, D_MODEL, c0, c0 + width)) + b_ref[:, c0:c0 + width]

        for c0 in range(0, D_A, COL_CHUNK):
            u = _proj(wa_ref, ba_ref, c0) * _sigmoid(_proj(wa_ref, ba_ref, D_A + c0))
            for j in range(COL_CHUNK // LANES):
                ubuf[:, c0 // LANES + j, HALO_A:HALO_A + lt, :] = (
                    u[:, j * LANES:(j + 1) * LANES].reshape(nseq, lt, LANES))
            ag_ref[:, c0:c0 + COL_CHUNK] = _silu(_proj(wa_ref, ba_ref, 2 * D_A + c0))
        for q in range(nseq):
            for lb in range(D_A // LANES):
                ue_ref[q, lb] = pltpu.bitcast(ubuf[q, lb, 0:HALO_A + lt, :].astype(BF16), jnp.uint32)
                uo_ref[q, lb] = pltpu.bitcast(ubuf[q, lb, 1:HALO_A + lt + 1, :].astype(BF16), jnp.uint32)

        def _conv_a_task(q, lb, r0):
            def run():
                cv_ref[q * lt + r0:q * lt + r0 + conv_rb, lb * LANES:(lb + 1) * LANES] = _conv_a_block(q, lb, r0)
            return run

        def _zx_task(c0):
            def run():
                res = _proj(wzx_ref, bzx_ref, c0)
                if c0 < D_INNER:
                    sz_ref[:, c0:c0 + COL_CHUNK] = _silu(res)
                else:
                    for j in range(COL_CHUNK // LANES):
                        xbuf[:, (c0 - D_INNER) // LANES + j, HALO_B:HALO_B + lt, :] = (
                            res[:, j * LANES:(j + 1) * LANES].reshape(nseq, lt, LANES))
            return run

        _run_interleaved(
            [_zx_task(c0) for c0 in range(0, D_INNER + D_XBC, COL_CHUNK)],
            [_conv_a_task(q, lb, r0) for q in range(nseq) for lb in range(D_A // LANES)
             for r0 in range(0, lt, conv_rb)])

        ln_rb = min(rows, CHUNK)

        def _norm_a_task(r0):
            def run():
                rs = slice(r0, r0 + ln_rb)
                va = _silu(_layer_norm(cv_ref[rs, :], nag_ref[...], nab_ref[...])) * ag_ref[rs, :]
                va_ref[rs, :] = va.astype(BF16)
            return run

        def _gate_task(c0):
            def run():
                sg_ref[:, c0:c0 + COL_CHUNK] = _sigmoid(_proj(wg_ref, bg_ref, c0))
            return run

        _run_interleaved([_gate_task(c0) for c0 in range(0, 3 * D_MODEL, COL_CHUNK)],
                         [_norm_a_task(r0) for r0 in range(0, rows, ln_rb)])

        def _conv_b_task(q, lb):
            def run():
                for r0 in range(0, lt, conv_rb):
                    xc_ref[q, r0:r0 + conv_rb, lb * LANES:(lb + 1) * LANES] = _silu(_dwconv_block(
                        xbuf, cbw_ref, cbb_ref, HALO_B, CONV_B_WIDTH, q, lb, r0))
            return run

        def _a_out_task(c0):
            def run():
                oa_ref[:, c0:c0 + COL_CHUNK] = _dot(va_ref[...], _w(waout_ref, 0, D_A, c0, c0 + COL_CHUNK))
            return run

        def _dt_task():
            dt4 = _softplus(_proj(wdt_ref, bdt_ref, 0, LANES) + dtb_ref[...])
            dt_ref[:, 0:lt, :] = dt4.reshape(nseq, lt, LANES)

        _run_interleaved([_a_out_task(c0) for c0 in range(0, D_MODEL, COL_CHUNK)] + [_dt_task],
                         [_conv_b_task(q, lb) for q in range(nseq) for lb in range(D_XBC // LANES)])
        if lp != lt:
            xc_ref[:, lt:lp, :] = jnp.zeros((nseq, lp - lt, D_XBC), F32)
            dt_ref[:, lt:lp, :] = jnp.zeros((nseq, lp - lt, LANES), F32)

    @pl.when(t == 0)
    def _init_state():
        if has_cache:
            ht_ref[...] = st_in[0].T
        else:
            ht_ref[...] = jnp.zeros(ht_ref.shape, F32)

    a_neg = -jnp.exp(alog_ref[...])
    lane_lo = lax.broadcasted_iota(jnp.int32, (1, LANES), 1) < CHUNK
    row_l = lax.broadcasted_iota(jnp.int32, (CHUNK, LANES), 0)
    col_s = lax.broadcasted_iota(jnp.int32, (CHUNK, LANES), 1) % CHUNK
    causal2 = row_l >= col_s
    bd_r = lax.broadcasted_iota(jnp.int32, (2 * CHUNK, LANES), 0) >= CHUNK
    bd_c = lax.broadcasted_iota(jnp.int32, (2 * CHUNK, LANES), 1) >= SSM_HEAD_DIM
    blockdiag = bd_r == bd_c
    lane4 = lax.broadcasted_iota(jnp.int32, (CHUNK, LANES), 1)
    tri3 = pltpu.bitcast(tri_ref[...], BF16)

    def _pieces(v4):
        v, r1, r2 = _split_residuals(v4)
        return jnp.where(lane4 < SSM_HEADS, v, jnp.where(lane4 < 2 * SSM_HEADS, r1,
                         jnp.where(lane4 < 3 * SSM_HEADS, r2, 0.0))).astype(BF16)

    k_acum, k_dt, k_dsd = [], [], []
    for c in range(nch):
        dtc = dt_ref[s, c * CHUNK:(c + 1) * CHUNK, :]
        v, r1, r2 = _split_residuals(dtc * a_neg)
        stack = jnp.concatenate([v.astype(BF16), r1.astype(BF16), r2.astype(BF16)], axis=0)
        acum = _dot(tri3, stack)
        at_ref[c] = jnp.concatenate([acum, acum], axis=0).T
        k_acum.append(_pieces(acum))
        k_dt.append(_pieces(dtc))
        k_dsd.append(_pieces(dtc * jnp.exp(acum[CHUNK - 1:CHUNK, :] - acum)))
    k_acum, k_dt, k_dsd = [jnp.concatenate(k, axis=0) if nch > 1 else k[0] for k in (k_acum, k_dt, k_dsd)]

    for g in range(SSM_GROUPS):
        gs = slice(g * GROUP_W, (g + 1) * GROUP_W)
        expand = _w(exp_ref, 0, LANES, g * GROUP_W, (g + 1) * GROUP_W)
        acum_e = _dot(k_acum, expand)
        expa_e = jnp.exp(acum_e)
        xs = xc_ref[s, :, gs]
        xdt = xs * _dot(k_dt, expand)
        xst = (xs * _dot(k_dsd, expand)).astype(BF16)
        h = ht_ref[:, gs]
        for c in range(nch):
            rs = slice(c * CHUNK, (c + 1) * CHUNK)
            bg = xc_ref[s, rs, D_INNER + g * SSM_STATE:D_INNER + (g + 1) * SSM_STATE].astype(BF16)
            cg = xc_ref[s, rs, D_INNER + (SSM_GROUPS + g) * SSM_STATE:
                        D_INNER + (SSM_GROUPS + g + 1) * SSM_STATE].astype(BF16)
            b2 = jnp.concatenate([bg, bg], axis=0)
            cb2 = lax.dot_general(cg, b2, (((1,), (1,)), ((), ())), preferred_element_type=F32)
            yoff = _dot(cg, h.astype(BF16))
            ydiag = []
            for j in range(PAIRS_PER_GROUP):
                pj = g * PAIRS_PER_GROUP + j
                cs = slice(j * LANES, (j + 1) * LANES)
                rowv = jnp.where(lane_lo, at_ref[c, 2 * pj:2 * pj + 1, :], at_ref[c, 2 * pj + 1:2 * pj + 2, :])
                lm = jnp.exp(jnp.where(causal2, acum_e[rs, cs] - rowv, -jnp.inf))
                mp = (cb2 * lm).astype(BF16)
                xp = xdt[rs, cs]
                xbd = jnp.where(blockdiag, jnp.concatenate([xp, xp], axis=0), 0.0).astype(BF16)
                ydiag.append(_dot(mp, xbd))
            decay = expa_e[rs, :]
            xc_ref[s, rs, gs] = (jnp.concatenate(ydiag, axis=1) + decay * yoff
                                 + dskip_ref[:, gs] * xs[rs, :])
            upd = lax.dot_general(bg, xst[rs, :], (((0,), (0,)), ((), ())), preferred_element_type=F32)
            h = h * decay[CHUNK - 1:CHUNK, :] + upd
        ht_ref[:, gs] = h

    @pl.when(t == nt - 1)
    def _final_state():
        st_ref[0] = ht_ref[...].T

    @pl.when(s == nseq - 1)
    def _dense_out():
        out_b = None
        for g in range(SSM_GROUPS):
            gs = slice(g * GROUP_W, (g + 1) * GROUP_W)
            gz = xc_ref[:, 0:lt, gs].reshape(rows, GROUP_W) * sz_ref[:, gs]
            gn = gz * lax.rsqrt(jnp.mean(gz * gz, axis=-1, keepdims=True) + LN_EPS) * gnw_ref[:, gs]
            part = _dot(gn.astype(BF16), _w(wbout_ref, g * GROUP_W, (g + 1) * GROUP_W, 0, D_MODEL))
            out_b = part if out_b is None else out_b + part

        merged = sg_ref[:, 0:D_MODEL] * oa_ref[...] + sg_ref[:, D_MODEL:2 * D_MODEL] * out_b
        pb = p_ref[...].reshape(rows, PLE_DIM).astype(BF16)
        ple = sg_ref[:, 2 * D_MODEL:3 * D_MODEL] * _dot(pb, _w(wple_ref, 0, PLE_DIM, 0, D_MODEL))
        r = (alpha * x_ref[...].reshape(rows, D_MODEL)
             + _dot(merged.astype(BF16), _w(wout_ref, 0, D_MODEL, 0, D_MODEL)) + ple)
        y_ref[...] = _layer_norm(r, lng_ref[...], lnb_ref[...]).reshape(nseq, lt, D_MODEL)

        @pl.when(t == nt - 1)
        def _finish():
            for lb in range(D_A // LANES):
                ca_ref[:, :, lb * LANES:(lb + 1) * LANES] = ubuf[:, lb, lt + HALO_A - hist_a:lt + HALO_A, :]
            for lb in range(D_XBC // LANES):
                cb_ref[:, :, lb * LANES:(lb + 1) * LANES] = xbuf[:, lb, lt + HALO_B - hist_b:lt + HALO_B, :]

        if nt > 1:
            ubuf[:, :, 0:HALO_A, :] = ubuf[:, :, lt:lt + HALO_A, :]
            xbuf[:, :, 0:HALO_B, :] = xbuf[:, :, lt:lt + HALO_B, :]


def _pack_kernel(w_ref, o_ref):
    o_ref[...] = pltpu.bitcast(w_ref[...].astype(BF16), jnp.uint32)


def _pack_weight(w, layer, col0, ncols):
    _, k, _ = w.shape
    bw = min(PACK_COLS, ncols)
    assert col0 % bw == 0 and ncols % bw == 0
    return pl.pallas_call(
        _pack_kernel,
        grid=(ncols // bw,),
        in_specs=[pl.BlockSpec((None, k, bw), lambda j: (layer, 0, col0 // bw + j))],
        out_specs=pl.BlockSpec((k // 2, bw), lambda j: (0, j)),
        out_shape=jax.ShapeDtypeStruct((k // 2, ncols), jnp.uint32),
        compiler_params=pltpu.CompilerParams(dimension_semantics=("arbitrary",)),
    )(w)


def _pack_rows_const(m):
    bits = np.where(m, 0x3F80, 0).astype(np.uint32)
    return jnp.asarray(bits[0::2, :] | (bits[1::2, :] << 16), jnp.uint32)


def _head_constants():
    k = np.arange(LANES)
    c = np.arange(D_INNER)
    expand = ((k[:, None] < 3 * SSM_HEADS) & ((k[:, None] % SSM_HEADS) == (c[None, :] // SSM_HEAD_DIM)))
    l = np.arange(CHUNK)
    kk = np.arange(3 * CHUNK)
    tri3 = (kk[None, :] % CHUNK) <= l[:, None]
    return _pack_rows_const(tri3), _pack_rows_const(expand)


def _layer_call(x, p, caches, lw, *, layer, nseq, lt, alpha):
    nb, seq, _ = x.shape
    nt = seq // lt
    assert nseq == 1 or nt == 1
    lp = -(-lt // CHUNK) * CHUNK
    rows = nseq * lt
    has_cache = caches is not None
    tri3, expand = _head_constants()
    consts = list(lw) + [tri3, expand]

    def tile_spec(width):
        return pl.BlockSpec((nseq, lt, width), lambda b, t, s: (b, t, 0))

    def tile_seq_spec(r, width):
        return pl.BlockSpec((nseq, r, width), lambda b, t, s: (b, 0, 0))

    def state_spec():
        return pl.BlockSpec((1, D_INNER, SSM_STATE), lambda b, t, s: (b * nseq + s, 0, 0))

    def layer_spec(block, index_map):
        return pl.BlockSpec((None,) + block, lambda b, t, s: (layer,) + index_map(b, t, s))

    def whole(a):
        return pl.BlockSpec(a.shape, lambda b, t, s: (0,) * a.ndim)

    in_specs = [tile_spec(D_MODEL), layer_spec((nseq, lt, PLE_DIM), lambda b, t, s: (b, t, 0))]
    args = [x, p]
    if has_cache:
        in_specs += [layer_spec((nseq, CONV_A_WIDTH - 1, D_A), lambda b, t, s: (b, 0, 0)),
                     layer_spec((nseq, CONV_B_WIDTH - 1, D_XBC), lambda b, t, s: (b, 0, 0)),
                     layer_spec((1, D_INNER, SSM_STATE), lambda b, t, s: (b * nseq + s, 0, 0))]
        args += list(caches)
    in_specs += [whole(a) for a in consts]
    args += consts
    out_shape = (jax.ShapeDtypeStruct((nb, seq, D_MODEL), F32),
                 jax.ShapeDtypeStruct((nb, CONV_A_WIDTH - 1, D_A), F32),
                 jax.ShapeDtypeStruct((nb, CONV_B_WIDTH - 1, D_XBC), F32),
                 jax.ShapeDtypeStruct((nb, D_INNER, SSM_STATE), F32))
    out_specs = (tile_spec(D_MODEL), tile_seq_spec(CONV_A_WIDTH - 1, D_A),
                 tile_seq_spec(CONV_B_WIDTH - 1, D_XBC), state_spec())
    scratch = [pltpu.VMEM((nseq, D_A // LANES, HALO_A + lt + TAIL_A, LANES), F32),
               pltpu.VMEM((nseq, D_A // LANES, (HALO_A + lt) // 2, LANES), jnp.uint32),
               pltpu.VMEM((nseq, D_A // LANES, (HALO_A + lt) // 2, LANES), jnp.uint32),
               pltpu.VMEM((nseq, D_XBC // LANES, HALO_B + lt, LANES), F32),
               pltpu.VMEM((rows, D_MODEL), BF16),
               pltpu.VMEM((rows, D_A), F32),
               pltpu.VMEM((rows, D_A), F32),
               pltpu.VMEM((rows, D_A), BF16),
               pltpu.VMEM((nseq, lp, D_XBC), F32),
               pltpu.VMEM((nseq, lp, LANES), F32),
               pltpu.VMEM((rows, D_INNER), F32),
               pltpu.VMEM((rows, 3 * D_MODEL), F32),
               pltpu.VMEM((rows, D_MODEL), F32),
               pltpu.VMEM((SSM_STATE, D_INNER), F32),
               pltpu.VMEM((lp // CHUNK, LANES, LANES), F32)]
    kern = functools.partial(_layer_kernel, nseq=nseq, lt=lt, nt=nt, has_cache=has_cache, alpha=alpha)
    return pl.pallas_call(
        kern,
        grid=(nb // nseq, nt, nseq),
        in_specs=in_specs,
        out_specs=out_specs,
        out_shape=out_shape,
        scratch_shapes=scratch,
        compiler_params=pltpu.CompilerParams(
            dimension_semantics=("arbitrary", "arbitrary", "arbitrary"),
            vmem_limit_bytes=VMEM_LIMIT_BYTES),
    )(*args)


O_Z = 3 * D_A
O_DT = O_Z + D_INNER + D_XBC
O_G = O_DT + SSM_HEADS


def _prep_layer_weights(i, w_dt_rep, w_gates, w_in, b_in, conv_a_w, conv_a_b, norm_a_g, norm_a_b, w_a_out,
                        conv_b_w, conv_b_b, dt_bias, a_log, d_skip, gnorm_w, w_b_out, w_out, w_ple, ln_g, ln_b):
    bi = b_in[i]
    row = lambda v: v.reshape(1, -1)
    rep = lambda v: jnp.tile(v, HEAD_REP).reshape(1, LANES)
    return [
        _pack_weight(w_in, i, 0, O_Z), row(bi[0:O_Z]),
        _pack_weight(w_in, i, O_Z, O_DT - O_Z), row(bi[O_Z:O_DT]),
        _pack_weight(w_dt_rep, i, 0, LANES), rep(bi[O_DT:O_G]),
        rep(dt_bias[i]), rep(a_log[i]),
        _pack_weight(w_gates, i, 0, 3 * D_MODEL), row(bi[O_G:]),
        conv_a_w[i], row(conv_a_b[i]), row(norm_a_g[i]), row(norm_a_b[i]), _pack_weight(w_a_out, i, 0, D_MODEL),
        conv_b_w[i], row(conv_b_b[i]), row(jnp.repeat(d_skip[i], SSM_HEAD_DIM)), row(gnorm_w[i]),
        _pack_weight(w_b_out, i, 0, D_MODEL), _pack_weight(w_out, i, 0, D_MODEL),
        _pack_weight(w_ple, i, 0, D_MODEL), row(ln_g[i]), row(ln_b[i]),
    ]


def kernel(x_prompt, x_sample, cache_conv_a, cache_conv_b, state_ssm, p_prompt, p_sample, w_in, b_in, conv_a_w, conv_a_b, norm_a_g, norm_a_b, w_a_out, conv_b_w, conv_b_b, dt_bias, a_log, d_skip, gnorm_w, w_b_out, w_out, w_ple, ln_g, ln_b):
    weights = (w_in, b_in, conv_a_w, conv_a_b, norm_a_g, norm_a_b, w_a_out, conv_b_w, conv_b_b,
               dt_bias, a_log, d_skip, gnorm_w, w_b_out, w_out, w_ple, ln_g, ln_b)
    depth = w_in.shape[0]
    alpha = float((2 * depth) ** 0.25)
    nb_s = x_sample.shape[0]
    caches = (cache_conv_a, cache_conv_b, state_ssm.reshape(depth, nb_s, D_INNER, SSM_STATE))
    w_dt_rep = jnp.tile(w_in[:, :, O_DT:O_G], (1, 1, HEAD_REP))
    w_gates = w_in[:, :, O_G:]
    xp, xs = x_prompt, x_sample
    outs_p, outs_s = [], []
    for i in range(depth):
        lw = _prep_layer_weights(i, w_dt_rep, w_gates, *weights)
        xp, *rest_p = _layer_call(xp, p_prompt, None, lw, layer=i, nseq=1, lt=256, alpha=alpha)
        xs, *rest_s = _layer_call(xs, p_sample, caches, lw, layer=i, nseq=4, lt=x_sample.shape[1], alpha=alpha)
        outs_p.append(rest_p)
        outs_s.append(rest_s)

    def stacked(outs, nb):
        ca, cb, st = (jnp.stack(v) for v in zip(*outs))
        return ca, cb, st.reshape(depth, nb, SSM_HEADS, SSM_HEAD_DIM, SSM_STATE)

    ca_p, cb_p, h_p = stacked(outs_p, x_prompt.shape[0])
    ca_s, cb_s, h_s = stacked(outs_s, nb_s)
    return (xp, xs, ca_p, cb_p, h_p, ca_s, cb_s, h_s)
```

```python
import functools

import numpy as np
import jax
import jax.numpy as jnp
from jax import lax
from jax.experimental import pallas as pl
from jax.experimental.pallas import tpu as pltpu

F32 = jnp.float32
BF16 = jnp.bfloat16

D_MODEL = 1024
D_A = 1024
CONV_A_WIDTH = 31
D_INNER = 2048
SSM_HEAD_DIM = 64
SSM_HEADS = 32
SSM_GROUPS = 4
SSM_STATE = 128
CONV_B_WIDTH = 4
D_XBC = D_INNER + 2 * SSM_GROUPS * SSM_STATE
PLE_DIM = 256
CHUNK = 64
LN_EPS = 1e-5

LANES = 128
HEAD_REP = LANES // SSM_HEADS
GROUP_W = D_INNER // SSM_GROUPS
PAIRS_PER_GROUP = GROUP_W // LANES
HALO_A = 32
TAIL_A = 8
HALO_B = 8
COL_CHUNK = 1024
PACK_COLS = 1024
VMEM_LIMIT_BYTES = 60 * 1024 * 1024


def _sigmoid(v):
    return jax.nn.sigmoid(v)


def _silu(v):
    return v * jax.nn.sigmoid(v)


def _softplus(v):
    return jnp.maximum(v, 0.0) + jnp.log1p(jnp.exp(-jnp.abs(v)))


def _layer_norm(v, g, b):
    mu = jnp.mean(v, axis=-1, keepdims=True)
    c = v - mu
    var = jnp.mean(c * c, axis=-1, keepdims=True)
    return c * lax.rsqrt(var + LN_EPS) * g + b


def _dot(a, b):
    return jnp.dot(a, b, preferred_element_type=F32)


def _w(ref, r0, r1, c0, c1):
    return pltpu.bitcast(ref[r0 // 2:r1 // 2, c0:c1], BF16)


def _split_residuals(v):
    r1 = v - v.astype(BF16).astype(F32)
    r2 = r1 - r1.astype(BF16).astype(F32)
    return v, r1, r2


def _layer_kernel(*refs, nseq, lt, nt, has_cache, alpha):
    it = iter(refs)
    x_ref, p_ref = next(it), next(it)
    if has_cache:
        ca_in, cb_in, st_in = next(it), next(it), next(it)
    (wa_ref, ba_ref, wzx_ref, bzx_ref, wdt_ref, bdt_ref, dtb_ref, alog_ref, wg_ref, bg_ref,
     caw_ref, cab_ref, nag_ref, nab_ref, waout_ref, cbw_ref, cbb_ref, dskip_ref, gnw_ref,
     wbout_ref, wout_ref, wple_ref, lng_ref, lnb_ref, tri_ref, exp_ref) = [next(it) for _ in range(26)]
    y_ref, ca_ref, cb_ref, st_ref = next(it), next(it), next(it), next(it)
    (ubuf, ue_ref, uo_ref, xbuf, xb_ref, cv_ref, ag_ref, va_ref, xc_ref, dt_ref, sz_ref, sg_ref, oa_ref,
     ht_ref, at_ref) = [next(it) for _ in range(15)]

    t = pl.program_id(1)
    s = pl.program_id(2)
    rows = nseq * lt
    lp = xc_ref.shape[1]
    nch = lp // CHUNK
    hist_a = CONV_A_WIDTH - 1
    hist_b = CONV_B_WIDTH - 1
    conv_rb = min(lt, CHUNK)

    def _dwconv_block(buf, w_ref, b_ref, halo, width, q, lb, r0):
        ls = slice(lb * LANES, (lb + 1) * LANES)
        acc = jnp.zeros((conv_rb, LANES), F32) + b_ref[:, ls]
        for k in range(width):
            off = halo - (width - 1) + k + r0
            acc = acc + buf[q, lb, off:off + conv_rb, :] * w_ref[k:k + 1, ls]
        return acc

    def _conv_a_block(q, lb, r0):
        ls = slice(lb * LANES, (lb + 1) * LANES)
        acc = jnp.zeros((conv_rb, LANES), BF16)
        for k in range(CONV_A_WIDTH):
            off = HALO_A - hist_a + k + r0
            src = ue_ref if off % 2 == 0 else uo_ref
            win = pltpu.bitcast(src[q, lb, off // 2:off // 2 + conv_rb // 2, :], BF16)
            acc = acc + win * caw_ref[k:k + 1, ls].astype(BF16)
        return acc.astype(F32) + cab_ref[:, ls]

    @pl.when(s == 0)
    def _dense_in():
        @pl.when(t == 0)
        def _init():
            ubuf[:, :, 0:HALO_A, :] = jnp.zeros((nseq, D_A // LANES, HALO_A, LANES), F32)
            ubuf[:, :, HALO_A + lt:HALO_A + lt + TAIL_A, :] = jnp.zeros((nseq, D_A // LANES, TAIL_A, LANES), F32)
            xbuf[:, :, 0:HALO_B, :] = jnp.zeros((nseq, D_XBC // LANES, HALO_B, LANES), F32)
            if has_cache:
                for lb in range(D_A // LANES):
                    ubuf[:, lb, HALO_A - hist_a:HALO_A, :] = ca_in[:, :, lb * LANES:(lb + 1) * LANES]
                for lb in range(D_XBC // LANES):
                    xbuf[:, lb, HALO_B - hist_b:HALO_B, :] = cb_in[:, :, lb * LANES:(lb + 1) * LANES]

        xb_ref[...] = x_ref[...].reshape(rows, D_MODEL).astype(BF16)

        def _proj(w_ref, b_ref, c0, width=COL_CHUNK):
            return _dot(xb_ref[...], _w(w_ref, 0, D_MODEL, c0, c0 + width)) + b_ref[:, c0:c0 + width]

        for c0 in range(0, D_A, COL_CHUNK):
            u = _proj(wa_ref, ba_ref, c0) * _sigmoid(_proj(wa_ref, ba_ref, D_A + c0))
            for j in range(COL_CHUNK // LANES):
                ubuf[:, c0 // LANES + j, HALO_A:HALO_A + lt, :] = (
                    u[:, j * LANES:(j + 1) * LANES].reshape(nseq, lt, LANES))
            ag_ref[:, c0:c0 + COL_CHUNK] = _silu(_proj(wa_ref, ba_ref, 2 * D_A + c0))
        for q in range(nseq):
            for lb in range(D_A // LANES):
                ue_ref[q, lb] = pltpu.bitcast(ubuf[q, lb, 0:HALO_A + lt, :].astype(BF16), jnp.uint32)
                uo_ref[q, lb] = pltpu.bitcast(ubuf[q, lb, 1:HALO_A + lt + 1, :].astype(BF16), jnp.uint32)

        for q in range(nseq):
            for lb in range(D_A // LANES):
                for r0 in range(0, lt, conv_rb):
                    cv_ref[q * lt + r0:q * lt + r0 + conv_rb, lb * LANES:(lb + 1) * LANES] = (
                        _conv_a_block(q, lb, r0))

        for c0 in range(0, D_INNER + D_XBC, COL_CHUNK):
            res = _proj(wzx_ref, bzx_ref, c0)
            if c0 < D_INNER:
                sz_ref[:, c0:c0 + COL_CHUNK] = _silu(res)
            else:
                for j in range(COL_CHUNK // LANES):
                    xbuf[:, (c0 - D_INNER) // LANES + j, HALO_B:HALO_B + lt, :] = (
                        res[:, j * LANES:(j + 1) * LANES].reshape(nseq, lt, LANES))

        ln_rb = min(rows, CHUNK)
        for r0 in range(0, rows, ln_rb):
            rs = slice(r0, r0 + ln_rb)
            va = _silu(_layer_norm(cv_ref[rs, :], nag_ref[...], nab_ref[...])) * ag_ref[rs, :]
            va_ref[rs, :] = va.astype(BF16)
        for c0 in range(0, 3 * D_MODEL, COL_CHUNK):
            sg_ref[:, c0:c0 + COL_CHUNK] = _sigmoid(_proj(wg_ref, bg_ref, c0))

        for q in range(nseq):
            for lb in range(D_XBC // LANES):
                for r0 in range(0, lt, conv_rb):
                    xc_ref[q, r0:r0 + conv_rb, lb * LANES:(lb + 1) * LANES] = _silu(_dwconv_block(
                        xbuf, cbw_ref, cbb_ref, HALO_B, CONV_B_WIDTH, q, lb, r0))
        for c0 in range(0, D_MODEL, COL_CHUNK):
            oa_ref[:, c0:c0 + COL_CHUNK] = _dot(va_ref[...], _w(waout_ref, 0, D_A, c0, c0 + COL_CHUNK))
        dt4 = _softplus(_proj(wdt_ref, bdt_ref, 0, LANES) + dtb_ref[...])
        dt_ref[:, 0:lt, :] = dt4.reshape(nseq, lt, LANES)
        if lp != lt:
            xc_ref[:, lt:lp, :] = jnp.zeros((nseq, lp - lt, D_XBC), F32)
            dt_ref[:, lt:lp, :] = jnp.zeros((nseq, lp - lt, LANES), F32)

    @pl.when(t == 0)
    def _init_state():
        if has_cache:
            ht_ref[...] = st_in[0].T
        else:
            ht_ref[...] = jnp.zeros(ht_ref.shape, F32)

    a_neg = -jnp.exp(alog_ref[...])
    lane_lo = lax.broadcasted_iota(jnp.int32, (1, LANES), 1) < CHUNK
    row_l = lax.broadcasted_iota(jnp.int32, (CHUNK, LANES), 0)
    col_s = lax.broadcasted_iota(jnp.int32, (CHUNK, LANES), 1) % CHUNK
    causal2 = row_l >= col_s
    bd_r = lax.broadcasted_iota(jnp.int32, (2 * CHUNK, LANES), 0) >= CHUNK
    bd_c = lax.broadcasted_iota(jnp.int32, (2 * CHUNK, LANES), 1) >= SSM_HEAD_DIM
    blockdiag = bd_r == bd_c
    lane4 = lax.broadcasted_iota(jnp.int32, (CHUNK, LANES), 1)
    tri3 = pltpu.bitcast(tri_ref[...], BF16)

    def _pieces(v4):
        v, r1, r2 = _split_residuals(v4)
        return jnp.where(lane4 < SSM_HEADS, v, jnp.where(lane4 < 2 * SSM_HEADS, r1,
                         jnp.where(lane4 < 3 * SSM_HEADS, r2, 0.0))).astype(BF16)

    k_acum, k_dt, k_dsd = [], [], []
    for c in range(nch):
        dtc = dt_ref[s, c * CHUNK:(c + 1) * CHUNK, :]
        v, r1, r2 = _split_residuals(dtc * a_neg)
        stack = jnp.concatenate([v.astype(BF16), r1.astype(BF16), r2.astype(BF16)], axis=0)
        acum = _dot(tri3, stack)
        at_ref[c] = jnp.concatenate([acum, acum], axis=0).T
        k_acum.append(_pieces(acum))
        k_dt.append(_pieces(dtc))
        k_dsd.append(_pieces(dtc * jnp.exp(acum[CHUNK - 1:CHUNK, :] - acum)))
    k_acum, k_dt, k_dsd = [jnp.concatenate(k, axis=0) if nch > 1 else k[0] for k in (k_acum, k_dt, k_dsd)]

    def _ssd_group(g):
        gs = slice(g * GROUP_W, (g + 1) * GROUP_W)
        expand = _w(exp_ref, 0, LANES, g * GROUP_W, (g + 1) * GROUP_W)
        acum_e = _dot(k_acum, expand)
        expa_e = jnp.exp(acum_e)
        xs = xc_ref[s, :, gs]
        xdt = xs * _dot(k_dt, expand)
        xst = (xs * _dot(k_dsd, expand)).astype(BF16)
        h = ht_ref[:, gs]
        for c in range(nch):
            rs = slice(c * CHUNK, (c + 1) * CHUNK)
            bg = xc_ref[s, rs, D_INNER + g * SSM_STATE:D_INNER + (g + 1) * SSM_STATE].astype(BF16)
            cg = xc_ref[s, rs, D_INNER + (SSM_GROUPS + g) * SSM_STATE:
                        D_INNER + (SSM_GROUPS + g + 1) * SSM_STATE].astype(BF16)
            b2 = jnp.concatenate([bg, bg], axis=0)
            cb2 = lax.dot_general(cg, b2, (((1,), (1,)), ((), ())), preferred_element_type=F32)
            yoff = _dot(cg, h.astype(BF16))
            ydiag = []
            for j in range(PAIRS_PER_GROUP):
                pj = g * PAIRS_PER_GROUP + j
                cs = slice(j * LANES, (j + 1) * LANES)
                rowv = jnp.where(lane_lo, at_ref[c, 2 * pj:2 * pj + 1, :], at_ref[c, 2 * pj + 1:2 * pj + 2, :])
                lm = jnp.exp(jnp.where(causal2, acum_e[rs, cs] - rowv, -jnp.inf))
                mp = (cb2 * lm).astype(BF16)
                xp = xdt[rs, cs]
                xbd = jnp.where(blockdiag, jnp.concatenate([xp, xp], axis=0), 0.0).astype(BF16)
                ydiag.append(_dot(mp, xbd))
            decay = expa_e[rs, :]
            xc_ref[s, rs, gs] = (jnp.concatenate(ydiag, axis=1) + decay * yoff
                                 + dskip_ref[:, gs] * xs[rs, :])
            upd = lax.dot_general(bg, xst[rs, :], (((0,), (0,)), ((), ())), preferred_element_type=F32)
            h = h * decay[CHUNK - 1:CHUNK, :] + upd
        ht_ref[:, gs] = h

    def _gated_norm(g):
        gs = slice(g * GROUP_W, (g + 1) * GROUP_W)
        gz = xc_ref[:, 0:lt, gs].reshape(rows, GROUP_W) * sz_ref[:, gs]
        gn = gz * lax.rsqrt(jnp.mean(gz * gz, axis=-1, keepdims=True) + LN_EPS) * gnw_ref[:, gs]
        return gn.astype(BF16)

    def _dense_tail(out_b):
        merged = sg_ref[:, 0:D_MODEL] * oa_ref[...] + sg_ref[:, D_MODEL:2 * D_MODEL] * out_b
        pb = p_ref[...].reshape(rows, PLE_DIM).astype(BF16)
        ple = sg_ref[:, 2 * D_MODEL:3 * D_MODEL] * _dot(pb, _w(wple_ref, 0, PLE_DIM, 0, D_MODEL))
        r = (alpha * x_ref[...].reshape(rows, D_MODEL)
             + _dot(merged.astype(BF16), _w(wout_ref, 0, D_MODEL, 0, D_MODEL)) + ple)
        y_ref[...] = _layer_norm(r, lng_ref[...], lnb_ref[...]).reshape(nseq, lt, D_MODEL)

        @pl.when(t == nt - 1)
        def _finish():
            for lb in range(D_A // LANES):
                ca_ref[:, :, lb * LANES:(lb + 1) * LANES] = ubuf[:, lb, lt + HALO_A - hist_a:lt + HALO_A, :]
            for lb in range(D_XBC // LANES):
                cb_ref[:, :, lb * LANES:(lb + 1) * LANES] = xbuf[:, lb, lt + HALO_B - hist_b:lt + HALO_B, :]

        if nt > 1:
            ubuf[:, :, 0:HALO_A, :] = ubuf[:, :, lt:lt + HALO_A, :]
            xbuf[:, :, 0:HALO_B, :] = xbuf[:, :, lt:lt + HALO_B, :]

    def _final_state():
        @pl.when(t == nt - 1)
        def _():
            st_ref[0] = ht_ref[...].T

    for g in range(SSM_GROUPS):
        _ssd_group(g)
    _final_state()

    @pl.when(s == nseq - 1)
    def _dense_out():
        gn = jnp.concatenate([_gated_norm(g) for g in range(SSM_GROUPS)], axis=1)
        _dense_tail(_dot(gn, _w(wbout_ref, 0, D_INNER, 0, D_MODEL)))


def _pack_kernel(w_ref, o_ref):
    o_ref[...] = pltpu.bitcast(w_ref[...].astype(BF16), jnp.uint32)


def _pack_weight(w, layer, col0, ncols):
    _, k, _ = w.shape
    bw = min(PACK_COLS, ncols)
    assert col0 % bw == 0 and ncols % bw == 0
    return pl.pallas_call(
        _pack_kernel,
        grid=(ncols // bw,),
        in_specs=[pl.BlockSpec((None, k, bw), lambda j: (layer, 0, col0 // bw + j))],
        out_specs=pl.BlockSpec((k // 2, bw), lambda j: (0, j)),
        out_shape=jax.ShapeDtypeStruct((k // 2, ncols), jnp.uint32),
        compiler_params=pltpu.CompilerParams(dimension_semantics=("arbitrary",)),
    )(w)


def _pack_rows_const(m):
    bits = np.where(m, 0x3F80, 0).astype(np.uint32)
    return jnp.asarray(bits[0::2, :] | (bits[1::2, :] << 16), jnp.uint32)


def _head_constants():
    k = np.arange(LANES)
    c = np.arange(D_INNER)
    expand = ((k[:, None] < 3 * SSM_HEADS) & ((k[:, None] % SSM_HEADS) == (c[None, :] // SSM_HEAD_DIM)))
    l = np.arange(CHUNK)
    kk = np.arange(3 * CHUNK)
    tri3 = (kk[None, :] % CHUNK) <= l[:, None]
    return _pack_rows_const(tri3), _pack_rows_const(expand)


def _layer_call(x, p, caches, lw, *, layer, nseq, lt, alpha):
    nb, seq, _ = x.shape
    nt = seq // lt
    assert nseq == 1 or nt == 1
    lp = -(-lt // CHUNK) * CHUNK
    rows = nseq * lt
    has_cache = caches is not None
    tri3, expand = _head_constants()
    consts = list(lw) + [tri3, expand]

    def tile_spec(width):
        return pl.BlockSpec((nseq, lt, width), lambda b, t, s: (b, t, 0))

    def tile_seq_spec(r, width):
        return pl.BlockSpec((nseq, r, width), lambda b, t, s: (b, 0, 0))

    def state_spec():
        return pl.BlockSpec((1, D_INNER, SSM_STATE), lambda b, t, s: (b * nseq + s, 0, 0))

    def layer_spec(block, index_map):
        return pl.BlockSpec((None,) + block, lambda b, t, s: (layer,) + index_map(b, t, s))

    def whole(a):
        return pl.BlockSpec(a.shape, lambda b, t, s: (0,) * a.ndim)

    in_specs = [tile_spec(D_MODEL), layer_spec((nseq, lt, PLE_DIM), lambda b, t, s: (b, t, 0))]
    args = [x, p]
    if has_cache:
        in_specs += [layer_spec((nseq, CONV_A_WIDTH - 1, D_A), lambda b, t, s: (b, 0, 0)),
                     layer_spec((nseq, CONV_B_WIDTH - 1, D_XBC), lambda b, t, s: (b, 0, 0)),
                     layer_spec((1, D_INNER, SSM_STATE), lambda b, t, s: (b * nseq + s, 0, 0))]
        args += list(caches)
    in_specs += [whole(a) for a in consts]
    args += consts
    out_shape = (jax.ShapeDtypeStruct((nb, seq, D_MODEL), F32),
                 jax.ShapeDtypeStruct((nb, CONV_A_WIDTH - 1, D_A), F32),
                 jax.ShapeDtypeStruct((nb, CONV_B_WIDTH - 1, D_XBC), F32),
                 jax.ShapeDtypeStruct((nb, D_INNER, SSM_STATE), F32))
    out_specs = (tile_spec(D_MODEL), tile_seq_spec(CONV_A_WIDTH - 1, D_A),
                 tile_seq_spec(CONV_B_WIDTH - 1, D_XBC), state_spec())
    scratch = [pltpu.VMEM((nseq, D_A // LANES, HALO_A + lt + TAIL_A, LANES), F32),
               pltpu.VMEM((nseq, D_A // LANES, (HALO_A + lt) // 2, LANES), jnp.uint32),
               pltpu.VMEM((nseq, D_A // LANES, (HALO_A + lt) // 2, LANES), jnp.uint32),
               pltpu.VMEM((nseq, D_XBC // LANES, HALO_B + lt, LANES), F32),
               pltpu.VMEM((rows, D_MODEL), BF16),
               pltpu.VMEM((rows, D_A), F32),
               pltpu.VMEM((rows, D_A), F32),
               pltpu.VMEM((rows, D_A), BF16),
               pltpu.VMEM((nseq, lp, D_XBC), F32),
               pltpu.VMEM((nseq, lp, LANES), F32),
               pltpu.VMEM((rows, D_INNER), F32),
               pltpu.VMEM((rows, 3 * D_MODEL), F32),
               pltpu.VMEM((rows, D_MODEL), F32),
               pltpu.VMEM((SSM_STATE, D_INNER), F32),
               pltpu.VMEM((lp // CHUNK, LANES, LANES), F32)]
    kern = functools.partial(_layer_kernel, nseq=nseq, lt=lt, nt=nt, has_cache=has_cache, alpha=alpha)
    return pl.pallas_call(
        kern,
        grid=(nb // nseq, nt, nseq),
        in_specs=in_specs,
        out_specs=out_specs,
        out_shape=out_shape,
        scratch_shapes=scratch,
        compiler_params=pltpu.CompilerParams(
            dimension_semantics=("arbitrary", "arbitrary", "arbitrary"),
            vmem_limit_bytes=VMEM_LIMIT_BYTES),
    )(*args)


O_Z = 3 * D_A
O_DT = O_Z + D_INNER + D_XBC
O_G = O_DT + SSM_HEADS


def _prep_layer_weights(i, w_dt_rep, w_gates, w_in, b_in, conv_a_w, conv_a_b, norm_a_g, norm_a_b, w_a_out,
                        conv_b_w, conv_b_b, dt_bias, a_log, d_skip, gnorm_w, w_b_out, w_out, w_ple, ln_g, ln_b):
    bi = b_in[i]
    row = lambda v: v.reshape(1, -1)
    rep = lambda v: jnp.tile(v, HEAD_REP).reshape(1, LANES)
    return [
        _pack_weight(w_in, i, 0, O_Z), row(bi[0:O_Z]),
        _pack_weight(w_in, i, O_Z, O_DT - O_Z), row(bi[O_Z:O_DT]),
        _pack_weight(w_dt_rep, i, 0, LANES), rep(bi[O_DT:O_G]),
        rep(dt_bias[i]), rep(a_log[i]),
        _pack_weight(w_gates, i, 0, 3 * D_MODEL), row(bi[O_G:]),
        conv_a_w[i], row(conv_a_b[i]), row(norm_a_g[i]), row(norm_a_b[i]), _pack_weight(w_a_out, i, 0, D_MODEL),
        conv_b_w[i], row(conv_b_b[i]), row(jnp.repeat(d_skip[i], SSM_HEAD_DIM)), row(gnorm_w[i]),
        _pack_weight(w_b_out, i, 0, D_MODEL), _pack_weight(w_out, i, 0, D_MODEL),
        _pack_weight(w_ple, i, 0, D_MODEL), row(ln_g[i]), row(ln_b[i]),
    ]


def kernel(x_prompt, x_sample, cache_conv_a, cache_conv_b, state_ssm, p_prompt, p_sample, w_in, b_in, conv_a_w, conv_a_b, norm_a_g, norm_a_b, w_a_out, conv_b_w, conv_b_b, dt_bias, a_log, d_skip, gnorm_w, w_b_out, w_out, w_ple, ln_g, ln_b):
    weights = (w_in, b_in, conv_a_w, conv_a_b, norm_a_g, norm_a_b, w_a_out, conv_b_w, conv_b_b,
               dt_bias, a_log, d_skip, gnorm_w, w_b_out, w_out, w_ple, ln_g, ln_b)
    depth = w_in.shape[0]
    alpha = float((2 * depth) ** 0.25)
    nb_s = x_sample.shape[0]
    caches = (cache_conv_a, cache_conv_b, state_ssm.reshape(depth, nb_s, D_INNER, SSM_STATE))
    w_dt_rep = jnp.tile(w_in[:, :, O_DT:O_G], (1, 1, HEAD_REP))
    w_gates = w_in[:, :, O_G:]
    xp, xs = x_prompt, x_sample
    outs_p, outs_s = [], []
    for i in range(depth):
        lw = _prep_layer_weights(i, w_dt_rep, w_gates, *weights)
        xp, *rest_p = _layer_call(xp, p_prompt, None, lw, layer=i, nseq=1, lt=256, alpha=alpha)
        xs, *rest_s = _layer_call(xs, p_sample, caches, lw, layer=i, nseq=4, lt=x_sample.shape[1], alpha=alpha)
        outs_p.append(rest_p)
        outs_s.append(rest_s)

    def stacked(outs, nb):
        ca, cb, st = (jnp.stack(v) for v in zip(*outs))
        return ca, cb, st.reshape(depth, nb, SSM_HEADS, SSM_HEAD_DIM, SSM_STATE)

    ca_p, cb_p, h_p = stacked(outs_p, x_prompt.shape[0])
    ca_s, cb_s, h_s = stacked(outs_s, nb_s)
    return (xp, xs, ca_p, cb_p, h_p, ca_s, cb_s, h_s)
```
